```python
import math
import jax, jax.numpy as jnp
from jax import lax
import numpy as np

D_MODEL = 1024
BATCH = 4
SEQ = 4096
DEPTH = 2

N_MIXERS = 2
MEM_LEN = 256
D_FF = 2752
LN_EPS = 1e-5
ALPHA = (2 * DEPTH) ** 0.25
BETA_INIT = (8 * DEPTH) ** -0.25
GDN_HEADS = 6
GDN_DK = 128
GDN_DV = 128
GDN_W = GDN_HEADS * GDN_DV
GDN_CONV = 4
GDN_CHUNK = 64
DSA_HEADS = 12
DSA_DH = 64
DSA_W = DSA_HEADS * DSA_DH
DSA_KV_RANK = 256
IDX_HEADS = 8
IDX_DIM = 64
TOPK_MAX = 256
Q_BLOCK = 128
MEM_HEADS = 4
MEM_DH = 64
MEM_W = MEM_HEADS * MEM_DH
GDN_IN = 4 * GDN_W + 2 * GDN_HEADS + MEM_W
DSA_IN = DSA_W + DSA_KV_RANK + IDX_HEADS * IDX_DIM + IDX_DIM + IDX_HEADS + MEM_W
MIX_OUT = GDN_W + MEM_W

kernel_name = "hybrid_gdn_dsa_macaron_deepnorm"


def layer_norm(x, g, b):
    xf = x.astype(jnp.float32)
    mu = jnp.mean(xf, -1, keepdims=True)
    var = jnp.mean(jnp.square(xf - mu), -1, keepdims=True)
    return ((xf - mu) * lax.rsqrt(var + LN_EPS) * g + b).astype(x.dtype)


def rms_norm(x, g):
    xf = x.astype(jnp.float32)
    return (xf * lax.rsqrt(jnp.mean(jnp.square(xf), -1, keepdims=True) + LN_EPS) * g).astype(x.dtype)


def l2_normalize(x):
    xf = x.astype(jnp.float32)
    return xf * lax.rsqrt(jnp.sum(jnp.square(xf), -1, keepdims=True) + 1e-6)


def post_norm_residual(x, y, g, b):
    return layer_norm(ALPHA * x + y, g, b)


def swiglu(x, w_in, w_out):
    gate, up = jnp.split(x @ w_in, 2, axis=-1)
    return (jax.nn.silu(gate) * up) @ w_out


def causal_dwconv(x, w):
    k, c = w.shape
    return lax.conv_general_dilated(x, w[:, None, :], window_strides=(1,), padding=((k - 1, 0),),
                                    dimension_numbers=("NWC", "WIO", "NWC"), feature_group_count=c)


def alibi_slopes(n_heads):
    return jnp.exp2(-8.0 * jnp.arange(1, n_heads + 1, dtype=jnp.float32) / n_heads)


def memory_attention(q_mem, mem, w_mem_kv):
    b, t, _ = q_mem.shape
    m = mem.shape[1]
    k, v = jnp.split(mem @ w_mem_kv, 2, axis=-1)
    q = q_mem.reshape(b, t, MEM_HEADS, MEM_DH)
    k = k.reshape(b, m, MEM_HEADS, MEM_DH)
    v = v.reshape(b, m, MEM_HEADS, MEM_DH)
    s = jnp.einsum("bthd,bmhd->bhtm", q, k).astype(jnp.float32) * (MEM_DH ** -0.5)
    p = jax.nn.softmax(s, axis=-1).astype(v.dtype)
    return jnp.einsum("bhtm,bmhd->bthd", p, v).reshape(b, t, MEM_W)


def gated_delta_rule(q, k, v, g, beta):
    b, t, h, dk = q.shape
    dv = v.shape[-1]
    c = GDN_CHUNK
    n = t // c
    f32 = jnp.float32

    def to_chunks(a):
        return jnp.swapaxes(a.reshape(b, n, c, *a.shape[2:]), 2, 3)

    q = to_chunks(q.astype(f32) * (dk ** -0.5))
    k = to_chunks(k.astype(f32))
    v = to_chunks(v.astype(f32))
    beta = to_chunks(beta.astype(f32))
    g = jnp.cumsum(to_chunks(g.astype(f32)), axis=-1)

    tri_incl = jnp.tril(jnp.ones((c, c), dtype=bool))
    tri_strict = jnp.tril(jnp.ones((c, c), dtype=bool), -1)
    gdiff = g[..., :, None] - g[..., None, :]
    decay = jnp.where(tri_incl, jnp.exp(jnp.where(tri_incl, gdiff, 0.0)), 0.0)

    kb = k * beta[..., None]
    a_mat = jnp.where(tri_strict, jnp.einsum("bnhid,bnhjd->bnhij", kb, k) * decay, 0.0)
    eye = jnp.eye(c, dtype=f32)
    t_mat = lax.linalg.triangular_solve(eye + a_mat, jnp.broadcast_to(eye, a_mat.shape),
                                        left_side=True, lower=True, unit_diagonal=True)
    u_base = jnp.einsum("bnhij,bnhje->bnhie", t_mat, v * beta[..., None])
    w_dec = jnp.einsum("bnhij,bnhjd->bnhid", t_mat, kb * jnp.exp(g)[..., None])
    qk = jnp.where(tri_incl, jnp.einsum("bnhid,bnhjd->bnhij", q, k) * decay, 0.0)
    q_dec = q * jnp.exp(g)[..., None]
    k_dec = k * jnp.exp(g[..., -1:] - g)[..., None]
    g_last = jnp.exp(g[..., -1])

    def step(state, inp):
        u_b, w_c, qk_c, qd, kd, gl = inp
        u = u_b - jnp.einsum("bhcd,bhde->bhce", w_c, state)
        o = jnp.einsum("bhcd,bhde->bhce", qd, state) + jnp.einsum("bhij,bhje->bhie", qk_c, u)
        state = state * gl[..., None, None] + jnp.einsum("bhcd,bhce->bhde", kd, u)
        return state, o

    xs = tuple(jnp.moveaxis(a, 1, 0) for a in (u_base, w_dec, qk, q_dec, k_dec, g_last))
    s0 = jnp.zeros((b, h, dk, dv), f32)
    _, o = lax.scan(step, s0, xs)
    o = jnp.swapaxes(jnp.moveaxis(o, 0, 1), 2, 3)
    return o.reshape(b, t, h, dv)


def gdn_mixer(x, mem, w_in, conv_w, a_log, dt_bias, onorm_g, w_mem_kv, w_o):
    b, t, _ = x.shape
    proj = x @ w_in
    qkv, gate, b_raw, a_raw, q_mem = jnp.split(
        proj, [3 * GDN_W, 4 * GDN_W, 4 * GDN_W + GDN_HEADS, 4 * GDN_W + 2 * GDN_HEADS], axis=-1)
    qkv = jax.nn.silu(causal_dwconv(qkv, conv_w))
    q, k, v = jnp.split(qkv, 3, axis=-1)
    q = l2_normalize(q.reshape(b, t, GDN_HEADS, GDN_DK))
    k = l2_normalize(k.reshape(b, t, GDN_HEADS, GDN_DK))
    v = v.reshape(b, t, GDN_HEADS, GDN_DV)
    beta = jax.nn.sigmoid(b_raw.astype(jnp.float32))
    g = -jnp.exp(a_log.astype(jnp.float32)) * jax.nn.softplus(a_raw.astype(jnp.float32) + dt_bias)
    o = gated_delta_rule(q, k, v, g, beta)
    o = rms_norm(o, onorm_g) * jax.nn.silu(gate.reshape(b, t, GDN_HEADS, GDN_DV).astype(jnp.float32))
    o = o.astype(x.dtype).reshape(b, t, GDN_W)
    o_mem = memory_attention(q_mem, mem, w_mem_kv)
    return jnp.concatenate([o, o_mem], axis=-1) @ w_o


def dsa_mixer(x, mem, w_in, kv_norm_g, w_uk, w_uv, kidx_ln_g, kidx_ln_b, w_mem_kv, w_o):
    b, t, _ = x.shape
    f32 = jnp.float32
    proj = x @ w_in
    o1 = DSA_W
    o2 = o1 + DSA_KV_RANK
    o3 = o2 + IDX_HEADS * IDX_DIM
    o4 = o3 + IDX_DIM
    o5 = o4 + IDX_HEADS
    q, c_kv, q_idx, k_idx, w_idx, q_mem = jnp.split(proj, [o1, o2, o3, o4, o5], axis=-1)
    c_kv = rms_norm(c_kv, kv_norm_g)
    k_idx = layer_norm(k_idx, kidx_ln_g, kidx_ln_b)
    w_idx = w_idx.astype(f32) * (IDX_HEADS ** -0.5 * IDX_DIM ** -0.5)
    topk = min(TOPK_MAX, t // 4)
    slopes = alibi_slopes(DSA_HEADS)
    nb = t // Q_BLOCK

    def blocks(a):
        return jnp.moveaxis(a.reshape(b, nb, Q_BLOCK, *a.shape[2:]), 1, 0)

    xs = (jnp.arange(nb, dtype=jnp.int32) * Q_BLOCK,
          blocks(q.reshape(b, t, DSA_HEADS, DSA_DH)),
          blocks(q_idx.reshape(b, t, IDX_HEADS, IDX_DIM)),
          blocks(w_idx))
    s_pos = jnp.arange(t, dtype=jnp.int32)

    def block(args):
        start, qb, qxb, wxb = args
        t_pos = start + jnp.arange(Q_BLOCK, dtype=jnp.int32)
        causal = s_pos[None, :] <= t_pos[:, None]
        isc = jax.nn.relu(jnp.einsum("bqhd,bsd->bqhs", qxb, k_idx).astype(f32))
        isc = jnp.einsum("bqhs,bqh->bqs", isc, wxb)
        isc = jnp.where(causal[None], isc, -jnp.inf)
        _, idx = lax.top_k(isc, topk)
        valid = idx <= t_pos[None, :, None]
        c_sel = jax.vmap(lambda cb, ib: cb[ib])(c_kv, idx)
        q_lat = jnp.einsum("bqhd,hdr->bqhr", qb, w_uk)
        s = jnp.einsum("bqhr,bqkr->bhqk", q_lat, c_sel).astype(f32) * (DSA_DH ** -0.5)
        dist = (t_pos[None, :, None] - idx).astype(f32)
        s = s - slopes[None, :, None, None] * dist[:, None]
        s = jnp.where(valid[:, None], s, -jnp.inf)
        p = jax.nn.softmax(s, axis=-1).astype(c_sel.dtype)
        o_lat = jnp.einsum("bhqk,bqkr->bqhr", p, c_sel)
        return jnp.einsum("bqhr,hrd->bqhd", o_lat, w_uv)

    o = lax.map(block, xs)
    o = jnp.moveaxis(o, 0, 1).reshape(b, t, DSA_W)
    o_mem = memory_attention(q_mem, mem, w_mem_kv)
    return jnp.concatenate([o, o_mem], axis=-1) @ w_o


def setup_inputs(seed: int = 0) -> dict:
    key = jax.random.key(seed)
    keys = iter(jax.random.split(key, 64))

    def nrm(shape, scale):
        return jax.random.normal(next(keys), shape, jnp.float32) * scale

    def gain(n):
        return 1.0 + nrm((n,), 0.02)

    def bias(n):
        return nrm((n,), 0.02)

    def ffn_in():
        return nrm((D_MODEL, 2 * D_FF), D_MODEL ** -0.5)

    def ffn_out():
        return nrm((D_FF, D_MODEL), D_FF ** -0.5 * BETA_INIT)

    dt = jnp.exp(jax.random.uniform(next(keys), (GDN_HEADS,), jnp.float32, math.log(1e-3), math.log(0.1)))
    inp = {}
    inp["x"] = nrm((BATCH, SEQ, D_MODEL), 1.0)
    inp["mem"] = nrm((BATCH, MEM_LEN, D_MODEL), 1.0)
    inp["l0_ffn1_w_in"] = ffn_in()
    inp["l0_ffn1_w_out"] = ffn_out()
    inp["l0_ln1_g"] = gain(D_MODEL)
    inp["l0_ln1_b"] = bias(D_MODEL)
    inp["l0_w_in"] = nrm((D_MODEL, GDN_IN), D_MODEL ** -0.5)
    inp["l0_conv_w"] = nrm((GDN_CONV, 3 * GDN_W), GDN_CONV ** -0.5)
    inp["l0_a_log"] = jnp.log(jax.random.uniform(next(keys), (GDN_HEADS,), jnp.float32, 1.0, 16.0))
    inp["l0_dt_bias"] = dt + jnp.log(-jnp.expm1(-dt))
    inp["l0_onorm_g"] = gain(GDN_DV)
    inp["l0_w_mem_kv"] = nrm((D_MODEL, 2 * MEM_W), D_MODEL ** -0.5)
    inp["l0_w_o"] = nrm((MIX_OUT, D_MODEL), MIX_OUT ** -0.5 * BETA_INIT)
    inp["l0_ln2_g"] = gain(D_MODEL)
    inp["l0_ln2_b"] = bias(D_MODEL)
    inp["l0_ffn2_w_in"] = ffn_in()
    inp["l0_ffn2_w_out"] = ffn_out()
    inp["l0_ln3_g"] = gain(D_MODEL)
    inp["l0_ln3_b"] = bias(D_MODEL)
    inp["l1_ffn1_w_in"] = ffn_in()
    inp["l1_ffn1_w_out"] = ffn_out()
    inp["l1_ln1_g"] = gain(D_MODEL)
    inp["l1_ln1_b"] = bias(D_MODEL)
    inp["l1_w_in"] = nrm((D_MODEL, DSA_IN), D_MODEL ** -0.5)
    inp["l1_kv_norm_g"] = gain(DSA_KV_RANK)
    inp["l1_w_uk"] = nrm((DSA_HEADS, DSA_DH, DSA_KV_RANK), DSA_KV_RANK ** -0.5)
    inp["l1_w_uv"] = nrm((DSA_HEADS, DSA_KV_RANK, DSA_DH), DSA_KV_RANK ** -0.5)
    inp["l1_kidx_ln_g"] = gain(IDX_DIM)
    inp["l1_kidx_ln_b"] = bias(IDX_DIM)
    inp["l1_w_mem_kv"] = nrm((D_MODEL, 2 * MEM_W), D_MODEL ** -0.5)
    inp["l1_w_o"] = nrm((MIX_OUT, D_MODEL), MIX_OUT ** -0.5 * BETA_INIT)
    inp["l1_ln2_g"] = gain(D_MODEL)
    inp["l1_ln2_b"] = bias(D_MODEL)
    inp["l1_ffn2_w_in"] = ffn_in()
    inp["l1_ffn2_w_out"] = ffn_out()
    inp["l1_ln3_g"] = gain(D_MODEL)
    inp["l1_ln3_b"] = bias(D_MODEL)
    return inp


def reference(x, mem,
              l0_ffn1_w_in, l0_ffn1_w_out, l0_ln1_g, l0_ln1_b,
              l0_w_in, l0_conv_w, l0_a_log, l0_dt_bias, l0_onorm_g, l0_w_mem_kv, l0_w_o, l0_ln2_g, l0_ln2_b,
              l0_ffn2_w_in, l0_ffn2_w_out, l0_ln3_g, l0_ln3_b,
              l1_ffn1_w_in, l1_ffn1_w_out, l1_ln1_g, l1_ln1_b,
              l1_w_in, l1_kv_norm_g, l1_w_uk, l1_w_uv, l1_kidx_ln_g, l1_kidx_ln_b, l1_w_mem_kv, l1_w_o,
              l1_ln2_g, l1_ln2_b,
              l1_ffn2_w_in, l1_ffn2_w_out, l1_ln3_g, l1_ln3_b):
    ffn_pre = ((l0_ffn1_w_in, l0_ffn1_w_out, l0_ln1_g, l0_ln1_b),
               (l1_ffn1_w_in, l1_ffn1_w_out, l1_ln1_g, l1_ln1_b))
    mixer_params = ((l0_w_in, l0_conv_w, l0_a_log, l0_dt_bias, l0_onorm_g, l0_w_mem_kv, l0_w_o),
                    (l1_w_in, l1_kv_norm_g, l1_w_uk, l1_w_uv, l1_kidx_ln_g, l1_kidx_ln_b, l1_w_mem_kv, l1_w_o))
    mixer_norm = ((l0_ln2_g, l0_ln2_b), (l1_ln2_g, l1_ln2_b))
    ffn_post = ((l0_ffn2_w_in, l0_ffn2_w_out, l0_ln3_g, l0_ln3_b),
                (l1_ffn2_w_in, l1_ffn2_w_out, l1_ln3_g, l1_ln3_b))
    for i in range(DEPTH):
        w1, w2, g, b = ffn_pre[i]
        x = post_norm_residual(x, 0.5 * swiglu(x, w1, w2), g, b)
        if i % N_MIXERS == 0:
            y = gdn_mixer(x, mem, *mixer_params[i])
        else:
            y = dsa_mixer(x, mem, *mixer_params[i])
        g, b = mixer_norm[i]
        x = post_norm_residual(x, y, g, b)
        w1, w2, g, b = ffn_post[i]
        x = post_norm_residual(x, 0.5 * swiglu(x, w1, w2), g, b)
    return x
```

```python
import functools
import math

import jax
import jax.numpy as jnp
from jax import lax
from jax.experimental import pallas as pl
from jax.experimental.pallas import tpu as pltpu

F32 = jnp.float32
BF16 = jnp.bfloat16
I32 = jnp.int32

DEPTH = 2
LN_EPS = 1e-5
ALPHA = (2 * DEPTH) ** 0.25
GDN_HEADS = 6
GDN_DK = 128
GDN_W = GDN_HEADS * GDN_DK
GDN_CONV = 4
DSA_HEADS = 12
DSA_DH = 64
DSA_W = DSA_HEADS * DSA_DH
DSA_KV_RANK = 256
IDX_HEADS = 8
IDX_DIM = 64
TOPK_MAX = 256
MEM_HEADS = 4
MEM_DH = 64
MEM_W = MEM_HEADS * MEM_DH

LANES = 128
SUBLANES = 8
VMEM_LIMIT_BYTES = 56 * 1024 * 1024

ROW_TILE = 512
FFN_CHUNK = 256
GDN_CHUNK = 128
GDN_TILE = 256
INV_BLOCK = 16
DSA_TQ = 128
DSA_TK = 256


def _cparams(sem):
    return pltpu.CompilerParams(dimension_semantics=sem, vmem_limit_bytes=VMEM_LIMIT_BYTES)


def _resident(shape):
    nd = len(shape)
    return pl.BlockSpec(shape, lambda *_: (0,) * nd, pipeline_mode=pl.Buffered(1))


def _mm(a, b):
    return jnp.dot(a.astype(BF16), b.astype(BF16), preferred_element_type=F32)


def _mm_nt(a, b):
    return lax.dot_general(a.astype(BF16), b.astype(BF16), (((1,), (1,)), ((), ())),
                           preferred_element_type=F32)


def _mm_tn(a, b):
    return lax.dot_general(a.astype(BF16), b.astype(BF16), (((0,), (0,)), ((), ())),
                           preferred_element_type=F32)


def _layer_norm(y, g, b):
    mu = jnp.mean(y, axis=-1, keepdims=True)
    d = y - mu
    var = jnp.mean(d * d, axis=-1, keepdims=True)
    return d * lax.rsqrt(var + LN_EPS) * g + b


def _silu(x):
    return x * jax.nn.sigmoid(x)


def _ffn_kernel(x_ref, wgu_ref, wo_ref, g_ref, b_ref, o_ref, acc_ref, *, n_chunks, fc):
    x = x_ref[...]
    xb = x.astype(BF16)
    for c in range(n_chunks):
        gu = jnp.dot(xb, wgu_ref[c], preferred_element_type=F32)
        h = (_silu(gu[:, :fc]) * gu[:, fc:]).astype(BF16)
        part = jnp.dot(h, wo_ref[c], preferred_element_type=F32)
        if c == 0:
            acc_ref[...] = part
        else:
            acc_ref[...] += part
    o_ref[...] = _layer_norm(ALPHA * x + 0.5 * acc_ref[...], g_ref[...], b_ref[...])


def _prep_ffn(w_in, w_out):
    d, two_f = w_in.shape
    dff = two_f // 2
    n = -(-dff // FFN_CHUNK)
    pad = n * FFN_CHUNK - dff
    wg = jnp.pad(w_in[:, :dff], ((0, 0), (0, pad))).reshape(d, n, FFN_CHUNK)
    wu = jnp.pad(w_in[:, dff:], ((0, 0), (0, pad))).reshape(d, n, FFN_CHUNK)
    wgu = jnp.concatenate([wg, wu], axis=2).transpose(1, 0, 2).astype(BF16)
    wo = jnp.pad(w_out, ((0, pad), (0, 0))).reshape(n, FFN_CHUNK, d).astype(BF16)
    return wgu, wo


def _ffn_ln(x2, w_in, w_out, g, b):
    m, d = x2.shape
    wgu, wo = _prep_ffn(w_in, w_out)
    n = wgu.shape[0]
    bm = min(ROW_TILE, m)
    row = pl.BlockSpec((bm, d), lambda i: (i, 0))
    return pl.pallas_call(
        functools.partial(_ffn_kernel, n_chunks=n, fc=FFN_CHUNK),
        grid=(m // bm,),
        in_specs=[row, _resident(wgu.shape), _resident(wo.shape), _resident((1, d)), _resident((1, d))],
        out_specs=row,
        out_shape=jax.ShapeDtypeStruct((m, d), F32),
        scratch_shapes=[pltpu.VMEM((bm, d), F32)],
        compiler_params=_cparams(("parallel",)),
        name="ffn_ln",
    )(x2, wgu, wo, g.reshape(1, d), b.reshape(1, d))


def _proj_kernel(x_ref, w_ref, *o_refs, widths):
    xb = x_ref[...].astype(BF16)
    off = 0
    for o_ref, n in zip(o_refs, widths):
        o_ref[...] = jnp.dot(xb, w_ref[:, off:off + n], preferred_element_type=F32).astype(o_ref.dtype)
        off += n


def _proj(x2, w, widths, dtypes):
    m, d = x2.shape
    bm = min(ROW_TILE, m)
    return pl.pallas_call(
        functools.partial(_proj_kernel, widths=widths),
        grid=(m // bm,),
        in_specs=[pl.BlockSpec((bm, d), lambda i: (i, 0)), _resident(w.shape)],
        out_specs=[pl.BlockSpec((bm, n), lambda i: (i, 0)) for n in widths],
        out_shape=[jax.ShapeDtypeStruct((m, n), dt) for n, dt in zip(widths, dtypes)],
        compiler_params=_cparams(("parallel",)),
        name="proj",
    )(x2, w)


def _pad_cols(w, n):
    return jnp.pad(w, ((0, 0), (0, n - w.shape[1])))


def _unit_lower_inverse(a, xor_idx, eye):
    c = a.shape[0]
    shift = int(math.log2(INV_BLOCK))
    ad = jnp.where((xor_idx >> shift) == 0, a, 0.0)
    p = eye - ad
    x = ad
    for _ in range(shift - 1):
        x = _mm(x, x)
        p = p + _mm(p, x)
    t = p
    while (1 << shift) < c:
        off = jnp.where((xor_idx >> shift) == 1, a, 0.0)
        t = t - _mm(_mm(t, off), t)
        shift += 1
    return t


def _gdn_kernel(alog_ref, dtb_ref, q_ref, k_ref, v_ref, gate_ref, small_ref,
                cwq_ref, cwk_ref, cwv_ref, og_ref, o_ref,
                s_ref, halo_ref, stage_ref, smt_ref, *, chunk, n_chunks):
    h = pl.program_id(1)
    tc = chunk * n_chunks

    @pl.when(pl.program_id(2) == 0)
    def _():
        s_ref[...] = jnp.zeros_like(s_ref)
        halo_ref[...] = jnp.zeros_like(halo_ref)

    def conv_silu(a, raw_ref, cw_ref):
        raw = raw_ref[0]
        stage_ref[a, 0:SUBLANES, :] = halo_ref[a]
        stage_ref[a, SUBLANES:SUBLANES + tc, :] = raw
        halo_ref[a] = raw[tc - SUBLANES:tc, :]
        cw = cw_ref[...]
        y = cw[GDN_CONV - 1:GDN_CONV, :] * raw
        for j in range(GDN_CONV - 1):
            lo = SUBLANES - (GDN_CONV - 1) + j
            y = y + cw[j:j + 1, :] * stage_ref[a, lo:lo + tc, :]
        return _silu(y)

    q = conv_silu(0, q_ref, cwq_ref)
    k = conv_silu(1, k_ref, cwk_ref)
    v = conv_silu(2, v_ref, cwv_ref)
    q = q * lax.rsqrt(jnp.sum(q * q, axis=-1, keepdims=True) + 1e-6) * (GDN_DK ** -0.5)
    k = k * lax.rsqrt(jnp.sum(k * k, axis=-1, keepdims=True) + 1e-6)

    smt_ref[...] = small_ref[0].T
    b_raw = jnp.broadcast_to(smt_ref[pl.ds(h, 1), :], (SUBLANES, tc))
    a_raw = jnp.broadcast_to(smt_ref[pl.ds(GDN_HEADS + h, 1), :], (SUBLANES, tc))
    beta_row = jax.nn.sigmoid(b_raw)
    a_log = jnp.full((SUBLANES, tc), alog_ref[h], F32)
    g_row = -jnp.exp(a_log) * jax.nn.softplus(a_raw + dtb_ref[h])
    lane_in_chunk = lax.broadcasted_iota(I32, (SUBLANES, tc), 1) & (chunk - 1)
    gc_row = g_row
    step = 1
    while step < chunk:
        gc_row = gc_row + jnp.where(lane_in_chunk >= step, pltpu.roll(gc_row, step, axis=1), 0.0)
        step *= 2

    def to_cols(row):
        return jnp.broadcast_to(row[0:1, :], (LANES, tc)).T

    beta_c = to_cols(beta_row)
    gc_c = to_cols(gc_row)

    ri = lax.broadcasted_iota(I32, (chunk, chunk), 0)
    ci = lax.broadcasted_iota(I32, (chunk, chunk), 1)
    incl = ri >= ci
    strict = ri > ci
    xor_idx = ri ^ ci
    eye = jnp.where(ri == ci, 1.0, 0.0).astype(F32)

    for c in range(n_chunks):
        rows = slice(c * chunk, (c + 1) * chunk)
        qc, kc, vc = q[rows], k[rows], v[rows]
        bc = beta_c[rows]
        gcc = gc_c[rows]
        gcr = gc_row[0:1, c * chunk:(c + 1) * chunk]
        decay = jnp.where(incl, jnp.exp(jnp.where(incl, gcc - gcr, 0.0)), 0.0)
        kb = kc * bc
        a_mat = jnp.where(strict, _mm_nt(kb, kc) * decay, 0.0)
        t_mat = _unit_lower_inverse(a_mat, xor_idx, eye)
        eg = jnp.exp(gcc)
        uw = _mm(t_mat, jnp.concatenate([vc * bc, kb * eg], axis=1))
        u_base, w_dec = uw[:, :GDN_DK], uw[:, GDN_DK:]
        qk = jnp.where(incl, _mm_nt(qc, kc) * decay, 0.0)
        g_last = gcc[chunk - 1:chunk, :]
        k_dec = kc * jnp.exp(g_last - gcc)
        state = s_ref[...]
        ws = _mm(jnp.concatenate([w_dec, qc * eg], axis=0), state)
        u = u_base - ws[:chunk]
        o = ws[chunk:] + _mm(qk, u)
        s_ref[...] = state * jnp.exp(g_last) + _mm_tn(k_dec, u)
        o = o * lax.rsqrt(jnp.mean(o * o, axis=-1, keepdims=True) + LN_EPS) * og_ref[...]
        o_ref[0, rows, :] = o * _silu(gate_ref[0, rows, :])


def _gdn_core(qkv, gate, small, conv_w, a_log, dt_bias, onorm_g):
    b, t, _ = qkv.shape
    tc = min(GDN_TILE, t)
    n_chunks = tc // GDN_CHUNK
    hh = GDN_HEADS

    def head_block(col0):
        return pl.BlockSpec((1, tc, GDN_DK), lambda bi, hi, ti: (bi, ti, col0 + hi))

    def conv_block(col0):
        return pl.BlockSpec((GDN_CONV, GDN_DK), lambda bi, hi, ti: (0, col0 + hi))

    smem = pl.BlockSpec(memory_space=pltpu.SMEM)
    return pl.pallas_call(
        functools.partial(_gdn_kernel, chunk=GDN_CHUNK, n_chunks=n_chunks),
        grid=(b, hh, t // tc),
        in_specs=[smem, smem,
                  head_block(0), head_block(hh), head_block(2 * hh),
                  head_block(0),
                  pl.BlockSpec((1, tc, LANES), lambda bi, hi, ti: (bi, ti, 0)),
                  conv_block(0), conv_block(hh), conv_block(2 * hh),
                  pl.BlockSpec((1, GDN_DK), lambda bi, hi, ti: (0, 0))],
        out_specs=head_block(0),
        out_shape=jax.ShapeDtypeStruct((b, t, GDN_W), F32),
        scratch_shapes=[pltpu.VMEM((GDN_DK, GDN_DK), F32),
                        pltpu.VMEM((3, SUBLANES, GDN_DK), F32),
                        pltpu.VMEM((3, SUBLANES + tc, GDN_DK), F32),
                        pltpu.VMEM((LANES, tc), F32)],
        compiler_params=_cparams(("parallel", "parallel", "arbitrary")),
        name="gdn_core",
    )(a_log, dt_bias, qkv, qkv, qkv, gate, small, conv_w, conv_w, conv_w, onorm_g.reshape(1, GDN_DK))


def _mix_out_kernel(x_ref, o_ref, qm_ref, kv_ref, wo1_ref, wo2_ref, g_ref, b_ref, out_ref):
    qm = qm_ref[0]
    kv = kv_ref[0]
    k_mem, v_mem = kv[:, :MEM_W], kv[:, MEM_W:]
    head_of_lane = lax.broadcasted_iota(I32, (1, MEM_W), 1) >> (MEM_DH.bit_length() - 1)
    o_mem = jnp.zeros(qm.shape, F32)
    for hd in range(MEM_HEADS):
        mine = head_of_lane == hd
        s = _mm_nt(jnp.where(mine, qm, 0.0), k_mem) * (MEM_DH ** -0.5)
        s = s - jnp.max(s, axis=-1, keepdims=True)
        p = jnp.exp(s)
        p = p / jnp.sum(p, axis=-1, keepdims=True)
        o_mem = o_mem + jnp.where(mine, _mm(p, v_mem), 0.0)
    y = _mm(o_ref[0], wo1_ref[...]) + _mm(o_mem, wo2_ref[...])
    out_ref[0] = _layer_norm(ALPHA * x_ref[0] + y, g_ref[...], b_ref[...])


def _mix_out(x3, o3, qm3, kv3, w_o, g, b):
    bsz, t, d = x3.shape
    w1 = o3.shape[-1]
    bm = min(ROW_TILE, t)
    wo1 = w_o[:w1].astype(BF16)
    wo2 = w_o[w1:].astype(BF16)

    def rows(n):
        return pl.BlockSpec((1, bm, n), lambda bi, ti: (bi, ti, 0))

    return pl.pallas_call(
        _mix_out_kernel,
        grid=(bsz, t // bm),
        in_specs=[rows(d), rows(w1), rows(MEM_W),
                  pl.BlockSpec((1,) + kv3.shape[1:], lambda bi, ti: (bi, 0, 0)),
                  _resident(wo1.shape), _resident(wo2.shape), _resident((1, d)), _resident((1, d))],
        out_specs=rows(d),
        out_shape=jax.ShapeDtypeStruct((bsz, t, d), F32),
        compiler_params=_cparams(("parallel", "parallel")),
        name="mix_out",
    )(x3, o3, qm3, kv3, wo1, wo2, g.reshape(1, d), b.reshape(1, d))


def _dsa_proj_kernel(x_ref, w_ref, kvg_ref, lng_ref, lnb_ref,
                     q_ref, ckv_ref, qidx_ref, kidx_ref, wsm_ref, qm_ref):
    xb = x_ref[...].astype(BF16)

    def cols(lo, n):
        return jnp.dot(xb, w_ref[:, lo:lo + n], preferred_element_type=F32)

    o1 = DSA_W
    o2 = o1 + DSA_KV_RANK
    o3 = o2 + IDX_HEADS * IDX_DIM
    q_ref[...] = cols(0, o1)
    c = cols(o1, DSA_KV_RANK)
    ckv_ref[...] = (c * lax.rsqrt(jnp.mean(c * c, axis=-1, keepdims=True) + LN_EPS)
                    * kvg_ref[...]).astype(ckv_ref.dtype)
    qidx_ref[...] = cols(o2, IDX_HEADS * IDX_DIM).astype(qidx_ref.dtype)
    lane = lax.broadcasted_iota(I32, (1, LANES), 1)
    for half in range(2):
        kx = cols(o3 + half * LANES, LANES)
        live = (lane >= half * IDX_DIM) & (lane < (half + 1) * IDX_DIM)
        mu = jnp.sum(kx, axis=-1, keepdims=True) * (1.0 / IDX_DIM)
        d = jnp.where(live, kx - mu, 0.0)
        var = jnp.sum(d * d, axis=-1, keepdims=True) * (1.0 / IDX_DIM)
        kn = d * lax.rsqrt(var + LN_EPS) * lng_ref[:, half * LANES:(half + 1) * LANES] \
            + lnb_ref[:, half * LANES:(half + 1) * LANES]
        kidx_ref[:, half * LANES:(half + 1) * LANES] = jnp.where(live, kn, 0.0).astype(kidx_ref.dtype)
    wsm_ref[...] = cols(o3 + 2 * LANES, LANES) * (IDX_HEADS ** -0.5 * IDX_DIM ** -0.5)
    qm_ref[...] = cols(o3 + 3 * LANES, MEM_W)


def _dsa_proj(x2, w_in, kv_norm_g, kidx_g, kidx_b):
    m, d = x2.shape
    o1 = DSA_W
    o2 = o1 + DSA_KV_RANK
    o3 = o2 + IDX_HEADS * IDX_DIM
    o4 = o3 + IDX_DIM
    o5 = o4 + IDX_HEADS
    zeros64 = jnp.zeros((d, IDX_DIM), w_in.dtype)
    w = jnp.concatenate([
        w_in[:, :o3],
        w_in[:, o3:o4], zeros64, zeros64, w_in[:, o3:o4],
        _pad_cols(w_in[:, o4:o5], LANES),
        w_in[:, o5:],
    ], axis=1).astype(BF16)
    z64 = jnp.zeros((IDX_DIM,), F32)
    lng = jnp.concatenate([kidx_g, z64, z64, kidx_g]).reshape(1, 2 * LANES)
    lnb = jnp.concatenate([kidx_b, z64, z64, kidx_b]).reshape(1, 2 * LANES)
    bm = min(ROW_TILE, m)
    widths = (DSA_W, DSA_KV_RANK, IDX_HEADS * IDX_DIM, 2 * LANES, LANES, MEM_W)
    dtypes = (F32, BF16, BF16, BF16, F32, F32)
    return pl.pallas_call(
        _dsa_proj_kernel,
        grid=(m // bm,),
        in_specs=[pl.BlockSpec((bm, d), lambda i: (i, 0)), _resident(w.shape),
                  _resident((1, DSA_KV_RANK)), _resident((1, 2 * LANES)), _resident((1, 2 * LANES))],
        out_specs=[pl.BlockSpec((bm, n), lambda i: (i, 0)) for n in widths],
        out_shape=[jax.ShapeDtypeStruct((m, n), dt) for n, dt in zip(widths, dtypes)],
        compiler_params=_cparams(("parallel",)),
        name="dsa_proj",
    )(x2, w, kv_norm_g.reshape(1, DSA_KV_RANK), lng, lnb)


def _ordered_key_to_f32(u):
    key = u ^ jnp.int32(-2 ** 31)
    bits = key ^ ((key >> 31) & jnp.int32(0x7FFFFFFF))
    return lax.bitcast_convert_type(bits, F32)


def _dsa_attn_kernel(q_ref, qidx_ref, wsm_ref, kidx_ref, ckv_ref, wuk_ref, wuv_ref, o_ref,
                     sc_ref, acc_ref, m_ref, l_ref, pos_ref, *, tq, tk, topk, pos_bits):
    hh, rr = DSA_HEADS, DSA_KV_RANK
    t0 = pl.program_id(1) * tq
    nk = (t0 + tq + tk - 1) // tk
    row = lax.broadcasted_iota(I32, (tq, tk), 0)
    col = lax.broadcasted_iota(I32, (tq, tk), 1)
    t_pos = t0 + row
    neg_inf = jnp.float32(-jnp.inf)

    w_sm = wsm_ref[0]
    w_cols = [w_sm[:, hd:hd + 1] for hd in range(IDX_HEADS)]
    pairs = IDX_HEADS // 2
    lhs = jnp.concatenate([qidx_ref[0, :, p * LANES:(p + 1) * LANES] for p in range(pairs)], axis=0)

    def idx_body(kb, carry):
        k0 = pl.multiple_of(kb * tk, tk)
        kblk = kidx_ref[0, pl.ds(k0, tk), :]
        acc = jnp.zeros((tq, tk), F32)
        for half in range(2):
            z = jnp.maximum(_mm_nt(lhs, kblk[:, half * LANES:(half + 1) * LANES]), 0.0)
            for p in range(pairs):
                acc = acc + z[p * tq:(p + 1) * tq] * w_cols[2 * p + half]
        sc_ref[kb] = jnp.where(k0 + col <= t_pos, acc, neg_inf)
        return carry

    lax.fori_loop(0, nk, idx_body, 0)

    k_row = jnp.minimum(topk, t0 + lax.broadcasted_iota(I32, (tq, 1), 0) + 1).astype(F32)

    def count(pred):
        def body(kb, acc):
            return acc + jnp.where(pred(sc_ref[kb], kb), 1.0, 0.0)
        return jnp.sum(lax.fori_loop(0, nk, body, jnp.zeros((tq, tk), F32)), axis=1, keepdims=True)

    def bit_body(it, tau_u):
        cand_u = tau_u | jnp.left_shift(jnp.int32(1), 31 - it)
        cand = _ordered_key_to_f32(cand_u)
        cnt = count(lambda blk, kb: blk >= cand)
        return jnp.where(cnt >= k_row, cand_u, tau_u)

    tau = _ordered_key_to_f32(lax.fori_loop(0, 32, bit_body, jnp.zeros((tq, 1), I32)))

    cnt_gt = count(lambda blk, kb: blk > tau)
    cnt_ge = count(lambda blk, kb: blk >= tau)
    need = k_row - cnt_gt
    pos_ref[...] = jnp.full((tq, 1), 2 ** 30, I32)

    @pl.when(jnp.max(cnt_ge - k_row) > 0.0)
    def _():
        def pos_body(it, pos):
            cand = pos | jnp.left_shift(jnp.int32(1), pos_bits - 1 - it)
            cnt = count(lambda blk, kb: (blk == tau) & (kb * tk + col < cand))
            return jnp.where(cnt < need, cand, pos)
        pos_ref[...] = lax.fori_loop(0, pos_bits, pos_body, jnp.zeros((tq, 1), I32))

    pos = pos_ref[...]

    q_lat = jnp.dot(q_ref[0].astype(BF16), wuk_ref[...], preferred_element_type=F32)
    qs = jnp.concatenate([q_lat[:, hd * rr:(hd + 1) * rr] for hd in range(hh)], axis=0)
    qs = (qs * (DSA_DH ** -0.5)).astype(BF16)
    slope_col = jnp.concatenate(
        [jnp.full((tq, 1), 2.0 ** (-8.0 * (hd + 1) / hh), F32) for hd in range(hh)], axis=0)
    m_ref[...] = jnp.full(m_ref.shape, neg_inf, F32)
    l_ref[...] = jnp.zeros(l_ref.shape, F32)
    acc_ref[...] = jnp.zeros(acc_ref.shape, F32)

    def att_body(kb, carry):
        k0 = pl.multiple_of(kb * tk, tk)
        cblk = ckv_ref[0, pl.ds(k0, tk), :]
        s = _mm_nt(qs, cblk)
        blk = sc_ref[kb]
        s_pos = k0 + col
        sel = (blk > tau) | ((blk == tau) & (s_pos <= pos))
        dist = (t_pos - s_pos).astype(F32)
        bias = jnp.where(sel, 0.0, neg_inf)
        s = s - slope_col * jnp.concatenate([dist] * hh, axis=0) + jnp.concatenate([bias] * hh, axis=0)
        m_old = m_ref[...]
        m_new = jnp.maximum(m_old, jnp.max(s, axis=1, keepdims=True))
        m_safe = jnp.where(m_new == neg_inf, 0.0, m_new)
        p = jnp.exp(s - m_safe)
        alpha = jnp.exp(m_old - m_safe)
        l_ref[...] = alpha * l_ref[...] + jnp.sum(p, axis=1, keepdims=True)
        acc_ref[...] = alpha * acc_ref[...] + jnp.dot(p.astype(BF16), cblk, preferred_element_type=F32)
        m_ref[...] = m_new
        return carry

    lax.fori_loop(0, nk, att_body, 0)
    o_lat = acc_ref[...] / l_ref[...]
    o_cat = jnp.concatenate([o_lat[hd * tq:(hd + 1) * tq] for hd in range(hh)], axis=1)
    o_ref[0] = jnp.dot(o_cat.astype(BF16), wuv_ref[...], preferred_element_type=F32)


def _block_diag(blocks):
    h, a, b = blocks.shape
    eye = jnp.eye(h, dtype=blocks.dtype)
    return (eye[:, None, :, None] * blocks[:, :, None, :]).reshape(h * a, h * b)


def _dsa_attn(q3, qidx3, wsm3, kidx3, ckv3, w_uk, w_uv):
    b, t, _ = q3.shape
    tq = min(DSA_TQ, t)
    tk = min(DSA_TK, t)
    topk = min(TOPK_MAX, t // 4)
    hh, rr = DSA_HEADS, DSA_KV_RANK
    wuk = _block_diag(w_uk).astype(BF16)
    wuv = _block_diag(w_uv).astype(BF16)

    def rows(n):
        return pl.BlockSpec((1, tq, n), lambda bi, ti: (bi, ti, 0))

    def whole(n):
        return pl.BlockSpec((1, t, n), lambda bi, ti: (bi, 0, 0))

    return pl.pallas_call(
        functools.partial(_dsa_attn_kernel, tq=tq, tk=tk, topk=topk,
                          pos_bits=max(1, (t - 1).bit_length())),
        grid=(b, t // tq),
        in_specs=[rows(DSA_W), rows(IDX_HEADS * IDX_DIM), rows(LANES),
                  whole(2 * LANES), whole(rr), _resident(wuk.shape), _resident(wuv.shape)],
        out_specs=rows(DSA_W),
        out_shape=jax.ShapeDtypeStruct((b, t, DSA_W), F32),
        scratch_shapes=[pltpu.VMEM((t // tk, tq, tk), F32),
                        pltpu.VMEM((hh * tq, rr), F32),
                        pltpu.VMEM((hh * tq, 1), F32),
                        pltpu.VMEM((hh * tq, 1), F32),
                        pltpu.VMEM((tq, 1), I32)],
        compiler_params=_cparams(("parallel", "arbitrary")),
        name="dsa_attn",
    )(q3, qidx3, wsm3, kidx3, ckv3, wuk, wuv)


def _mem_kv(mem, w_mem_kv):
    b, mlen, d = mem.shape
    (kv,) = _proj(mem.reshape(b * mlen, d), w_mem_kv.astype(BF16), (2 * MEM_W,), (BF16,))
    return kv.reshape(b, mlen, 2 * MEM_W)


def _gdn_layer(x3, mem, w_in, conv_w, a_log, dt_bias, onorm_g, w_mem_kv, w_o, g, b):
    bsz, t, d = x3.shape
    n_qkv = 3 * GDN_W
    n_gate = 4 * GDN_W
    n_small = n_gate + 2 * GDN_HEADS
    w = jnp.concatenate([w_in[:, :n_gate], _pad_cols(w_in[:, n_gate:n_small], LANES),
                         w_in[:, n_small:]], axis=1).astype(BF16)
    qkv, gate, small, q_mem = _proj(x3.reshape(bsz * t, d), w, (n_qkv, GDN_W, LANES, MEM_W),
                                    (F32, F32, F32, F32))
    o = _gdn_core(qkv.reshape(bsz, t, n_qkv), gate.reshape(bsz, t, GDN_W), small.reshape(bsz, t, LANES),
                  conv_w, a_log, dt_bias, onorm_g)
    return _mix_out(x3, o, q_mem.reshape(bsz, t, MEM_W), _mem_kv(mem, w_mem_kv), w_o, g, b)


def _dsa_layer(x3, mem, w_in, kv_norm_g, w_uk, w_uv, kidx_g, kidx_b, w_mem_kv, w_o, g, b):
    bsz, t, d = x3.shape
    q, ckv, qidx, kidx, wsm, q_mem = _dsa_proj(x3.reshape(bsz * t, d), w_in, kv_norm_g, kidx_g, kidx_b)

    def r3(a):
        return a.reshape(bsz, t, a.shape[-1])

    o = _dsa_attn(r3(q), r3(qidx), r3(wsm), r3(kidx), r3(ckv), w_uk, w_uv)
    return _mix_out(x3, o, r3(q_mem), _mem_kv(mem, w_mem_kv), w_o, g, b)


def kernel(x, mem, l0_ffn1_w_in, l0_ffn1_w_out, l0_ln1_g, l0_ln1_b, l0_w_in, l0_conv_w, l0_a_log, l0_dt_bias, l0_onorm_g, l0_w_mem_kv, l0_w_o, l0_ln2_g, l0_ln2_b, l0_ffn2_w_in, l0_ffn2_w_out, l0_ln3_g, l0_ln3_b, l1_ffn1_w_in, l1_ffn1_w_out, l1_ln1_g, l1_ln1_b, l1_w_in, l1_kv_norm_g, l1_w_uk, l1_w_uv, l1_kidx_ln_g, l1_kidx_ln_b, l1_w_mem_kv, l1_w_o, l1_ln2_g, l1_ln2_b, l1_ffn2_w_in, l1_ffn2_w_out, l1_ln3_g, l1_ln3_b):
    bsz, t, d = x.shape

    def ffn(h3, w_in, w_out, g, b):
        return _ffn_ln(h3.reshape(bsz * t, d), w_in, w_out, g, b).reshape(bsz, t, d)

    h = ffn(x, l0_ffn1_w_in, l0_ffn1_w_out, l0_ln1_g, l0_ln1_b)
    h = _gdn_layer(h, mem, l0_w_in, l0_conv_w, l0_a_log, l0_dt_bias, l0_onorm_g, l0_w_mem_kv, l0_w_o,
                   l0_ln2_g, l0_ln2_b)
    h = ffn(h, l0_ffn2_w_in, l0_ffn2_w_out, l0_ln3_g, l0_ln3_b)
    h = ffn(h, l1_ffn1_w_in, l1_ffn1_w_out, l1_ln1_g, l1_ln1_b)
    h = _dsa_layer(h, mem, l1_w_in, l1_kv_norm_g, l1_w_uk, l1_w_uv, l1_kidx_ln_g, l1_kidx_ln_b,
                   l1_w_mem_kv, l1_w_o, l1_ln2_g, l1_ln2_b)
    h = ffn(h, l1_ffn2_w_in, l1_ffn2_w_out, l1_ln3_g, l1_ln3_b)
    return h
```

```python
import functools
import math

import jax
import jax.numpy as jnp
from jax import lax
from jax.experimental import pallas as pl
from jax.experimental.pallas import tpu as pltpu

F32 = jnp.float32
BF16 = jnp.bfloat16
I32 = jnp.int32

DEPTH = 2
LN_EPS = 1e-5
ALPHA = (2 * DEPTH) ** 0.25
GDN_HEADS = 6
GDN_DK = 128
GDN_W = GDN_HEADS * GDN_DK
GDN_CONV = 4
DSA_HEADS = 12
DSA_DH = 64
DSA_W = DSA_HEADS * DSA_DH
DSA_KV_RANK = 256
IDX_HEADS = 8
IDX_DIM = 64
TOPK_MAX = 256
MEM_HEADS = 4
MEM_DH = 64
MEM_W = MEM_HEADS * MEM_DH

LANES = 128
SUBLANES = 8
VMEM_LIMIT_BYTES = 56 * 1024 * 1024

ROW_TILE = 512
FFN_CHUNK = 256
GDN_CHUNK = 128
GDN_TILE = 256
INV_BLOCK = 16
DSA_TQ = LANES
DSA_TK = 512
NO_INDEX_BOUND = 2 ** 30
COUNT_ROWS = 8 * SUBLANES
SUM_ROWS = 2 * SUBLANES
LOG2E = math.log2(math.e)


def _cparams(sem):
    return pltpu.CompilerParams(dimension_semantics=sem, vmem_limit_bytes=VMEM_LIMIT_BYTES)


def _resident(shape):
    nd = len(shape)
    return pl.BlockSpec(shape, lambda *_: (0,) * nd, pipeline_mode=pl.Buffered(1))


def _mm(a, b):
    return jnp.dot(a.astype(BF16), b.astype(BF16), preferred_element_type=F32)


def _mm_nt(a, b):
    return lax.dot_general(a.astype(BF16), b.astype(BF16), (((1,), (1,)), ((), ())),
                           preferred_element_type=F32)


def _mm_tn(a, b):
    return lax.dot_general(a.astype(BF16), b.astype(BF16), (((0,), (0,)), ((), ())),
                           preferred_element_type=F32)


def _layer_norm(y, g, b):
    mu = jnp.mean(y, axis=-1, keepdims=True)
    d = y - mu
    var = jnp.mean(d * d, axis=-1, keepdims=True)
    return d * lax.rsqrt(var + LN_EPS) * g + b


def _silu(x):
    return x * jax.nn.sigmoid(x)


def _ffn_kernel(x_ref, wgu_ref, wo_ref, g_ref, b_ref, o_ref, acc_ref, *, n_chunks, fc):
    x = x_ref[...]
    xb = x.astype(BF16)
    for c in range(n_chunks):
        gu = jnp.dot(xb, wgu_ref[c], preferred_element_type=F32)
        h = (_silu(gu[:, :fc]) * gu[:, fc:]).astype(BF16)
        part = jnp.dot(h, wo_ref[c], preferred_element_type=F32)
        if c == 0:
            acc_ref[...] = part
        else:
            acc_ref[...] += part
    o_ref[...] = _layer_norm(ALPHA * x + 0.5 * acc_ref[...], g_ref[...], b_ref[...])


def _prep_ffn(w_in, w_out):
    d, two_f = w_in.shape
    dff = two_f // 2
    n = -(-dff // FFN_CHUNK)
    pad = n * FFN_CHUNK - dff
    wg = jnp.pad(w_in[:, :dff], ((0, 0), (0, pad))).reshape(d, n, FFN_CHUNK)
    wu = jnp.pad(w_in[:, dff:], ((0, 0), (0, pad))).reshape(d, n, FFN_CHUNK)
    wgu = jnp.concatenate([wg, wu], axis=2).transpose(1, 0, 2).astype(BF16)
    wo = jnp.pad(w_out, ((0, pad), (0, 0))).reshape(n, FFN_CHUNK, d).astype(BF16)
    return wgu, wo


def _ffn_ln(x2, w_in, w_out, g, b):
    m, d = x2.shape
    wgu, wo = _prep_ffn(w_in, w_out)
    n = wgu.shape[0]
    bm = min(ROW_TILE, m)
    row = pl.BlockSpec((bm, d), lambda i: (i, 0))
    return pl.pallas_call(
        functools.partial(_ffn_kernel, n_chunks=n, fc=FFN_CHUNK),
        grid=(m // bm,),
        in_specs=[row, _resident(wgu.shape), _resident(wo.shape), _resident((1, d)), _resident((1, d))],
        out_specs=row,
        out_shape=jax.ShapeDtypeStruct((m, d), F32),
        scratch_shapes=[pltpu.VMEM((bm, d), F32)],
        compiler_params=_cparams(("parallel",)),
        name="ffn_ln",
    )(x2, wgu, wo, g.reshape(1, d), b.reshape(1, d))


def _proj_kernel(x_ref, w_ref, *o_refs, widths):
    xb = x_ref[...].astype(BF16)
    off = 0
    for o_ref, n in zip(o_refs, widths):
        o_ref[...] = jnp.dot(xb, w_ref[:, off:off + n], preferred_element_type=F32).astype(o_ref.dtype)
        off += n


def _proj(x2, w, widths, dtypes):
    m, d = x2.shape
    bm = min(ROW_TILE, m)
    return pl.pallas_call(
        functools.partial(_proj_kernel, widths=widths),
        grid=(m // bm,),
        in_specs=[pl.BlockSpec((bm, d), lambda i: (i, 0)), _resident(w.shape)],
        out_specs=[pl.BlockSpec((bm, n), lambda i: (i, 0)) for n in widths],
        out_shape=[jax.ShapeDtypeStruct((m, n), dt) for n, dt in zip(widths, dtypes)],
        compiler_params=_cparams(("parallel",)),
        name="proj",
    )(x2, w)


def _pad_cols(w, n):
    return jnp.pad(w, ((0, 0), (0, n - w.shape[1])))


def _unit_lower_inverse(mats, xor_idx, eye):
    c = mats[0].shape[0]
    shift = int(math.log2(INV_BLOCK))
    x = [jnp.where((xor_idx >> shift) == 0, a, 0.0) for a in mats]
    p = [eye - xi for xi in x]
    for _ in range(shift - 1):
        x = [_mm(xi, xi) for xi in x]
        p = [pi + _mm(pi, xi) for pi, xi in zip(p, x)]
    t = p
    while (1 << shift) < c:
        off = [jnp.where((xor_idx >> shift) == 1, a, 0.0) for a in mats]
        t_off = [_mm(ti, oi) for ti, oi in zip(t, off)]
        t = [ti - _mm(toi, ti) for ti, toi in zip(t, t_off)]
        shift += 1
    return t


def _gdn_kernel(alog_ref, dtb_ref, q_ref, k_ref, v_ref, gate_ref, small_ref,
                cwq_ref, cwk_ref, cwv_ref, og_ref, o_ref,
                s_ref, halo_ref, stage_ref, smt_ref, *, chunk, n_chunks):
    tc = chunk * n_chunks

    @pl.when(pl.program_id(1) == 0)
    def _():
        s_ref[...] = jnp.zeros_like(s_ref)
        halo_ref[...] = jnp.zeros_like(halo_ref)

    def conv_silu(a, raw_ref, cw_ref):
        raw = raw_ref[0]
        stage_ref[a, 0:SUBLANES, :] = halo_ref[a]
        stage_ref[a, SUBLANES:SUBLANES + tc, :] = raw
        halo_ref[a] = raw[tc - SUBLANES:tc, :]
        cw = cw_ref[...]
        y = cw[GDN_CONV - 1:GDN_CONV, :] * raw
        for j in range(GDN_CONV - 1):
            lo = SUBLANES - (GDN_CONV - 1) + j
            y = y + cw[j:j + 1, :] * stage_ref[a, lo:lo + tc, :]
        return _silu(y)

    q_all = conv_silu(0, q_ref, cwq_ref)
    k_all = conv_silu(1, k_ref, cwk_ref)
    v_all = conv_silu(2, v_ref, cwv_ref)

    smt_ref[...] = small_ref[0].T
    lane_in_chunk = lax.broadcasted_iota(I32, (SUBLANES, tc), 1) & (chunk - 1)

    def to_cols(row):
        return jnp.broadcast_to(row[0:1, :], (LANES, tc)).T

    ri = lax.broadcasted_iota(I32, (chunk, chunk), 0)
    ci = lax.broadcasted_iota(I32, (chunk, chunk), 1)
    incl = ri >= ci
    strict = ri > ci
    xor_idx = ri ^ ci
    eye = jnp.where(ri == ci, 1.0, 0.0).astype(F32)

    def each(f, *lists):
        return [f(*args) for args in zip(*lists)]

    heads = list(range(GDN_HEADS))
    col_of = [slice(h * GDN_DK, (h + 1) * GDN_DK) for h in heads]
    q_h = [q_all[:, cs] for cs in col_of]
    k_h = [k_all[:, cs] for cs in col_of]
    v_h = [v_all[:, cs] for cs in col_of]
    q_h = each(lambda q: q * lax.rsqrt(jnp.sum(q * q, axis=-1, keepdims=True) + 1e-6) * (GDN_DK ** -0.5), q_h)
    k_h = each(lambda k: k * lax.rsqrt(jnp.sum(k * k, axis=-1, keepdims=True) + 1e-6), k_h)
    beta_row = [jax.nn.sigmoid(jnp.broadcast_to(smt_ref[h:h + 1, :], (SUBLANES, tc))) for h in heads]
    g_row = [-jnp.exp(jnp.full((SUBLANES, tc), alog_ref[h], F32))
             * jax.nn.softplus(jnp.broadcast_to(smt_ref[GDN_HEADS + h:GDN_HEADS + h + 1, :], (SUBLANES, tc))
                               + dtb_ref[h]) for h in heads]
    gc_row = g_row
    step = 1
    while step < chunk:
        gc_row = each(lambda g: g + jnp.where(lane_in_chunk >= step, pltpu.roll(g, step, axis=1), 0.0), gc_row)
        step *= 2
    beta_c = each(to_cols, beta_row)
    gc_c = each(to_cols, gc_row)

    probs = [(c, h) for c in range(n_chunks) for h in heads]
    rows_of = [slice(c * chunk, (c + 1) * chunk) for c, _ in probs]
    qc = [q_h[h][rows_of[i]] for i, (_, h) in enumerate(probs)]
    kc = [k_h[h][rows_of[i]] for i, (_, h) in enumerate(probs)]
    vc = [v_h[h][rows_of[i]] for i, (_, h) in enumerate(probs)]
    bc = [beta_c[h][rows_of[i]] for i, (_, h) in enumerate(probs)]
    gcc = [gc_c[h][rows_of[i]] for i, (_, h) in enumerate(probs)]
    gcr = [gc_row[h][0:1, rows_of[i]] for i, (_, h) in enumerate(probs)]
    decay = each(lambda a, b: jnp.where(incl, jnp.exp(jnp.where(incl, a - b, 0.0)), 0.0), gcc, gcr)
    kb = each(lambda a, b: a * b, kc, bc)
    a_mat = each(lambda a, b, d: jnp.where(strict, _mm_nt(a, b) * d, 0.0), kb, kc, decay)
    t_mat = _unit_lower_inverse(a_mat, xor_idx, eye)
    eg = each(jnp.exp, gcc)
    uw = each(lambda t, v, b, kbi, e: _mm(t, jnp.concatenate([v * b, kbi * e], axis=1)), t_mat, vc, bc, kb, eg)
    qk = each(lambda a, b, d: jnp.where(incl, _mm_nt(a, b) * d, 0.0), qc, kc, decay)
    g_last = each(lambda g: g[chunk - 1:chunk, :], gcc)
    k_dec = each(lambda k, gl, g: k * jnp.exp(gl - g), kc, g_last, gcc)
    lhs = each(lambda w, q, e: jnp.concatenate([w[:, GDN_DK:], q * e], axis=0), uw, qc, eg)

    state = [s_ref[h] for h in heads]
    for c in range(n_chunks):
        idx = [c * GDN_HEADS + h for h in heads]
        ws = [_mm(lhs[i], state[h]) for i, h in zip(idx, heads)]
        u = [uw[i][:, :GDN_DK] - w[:chunk] for i, w in zip(idx, ws)]
        o = [w[chunk:] + _mm(qk[i], ui) for i, w, ui in zip(idx, ws, u)]
        state = [state[h] * jnp.exp(g_last[i]) + _mm_tn(k_dec[i], ui) for i, h, ui in zip(idx, heads, u)]
        o = each(lambda x: x * lax.rsqrt(jnp.mean(x * x, axis=-1, keepdims=True) + LN_EPS) * og_ref[...], o)
        for h in heads:
            o_ref[0, rows_of[idx[h]], col_of[h]] = o[h] * _silu(gate_ref[0, rows_of[idx[h]], col_of[h]])
    for h in heads:
        s_ref[h] = state[h]


def _gdn_core(qkv, gate, small, conv_w, a_log, dt_bias, onorm_g):
    b, t, _ = qkv.shape
    tc = min(GDN_TILE, t)
    n_chunks = tc // GDN_CHUNK
    hh = GDN_HEADS

    def tok_block(j):
        return pl.BlockSpec((1, tc, GDN_W), lambda bi, ti: (bi, ti, j))

    def conv_block(j):
        return pl.BlockSpec((GDN_CONV, GDN_W), lambda bi, ti: (0, j))

    smem = pl.BlockSpec(memory_space=pltpu.SMEM)
    return pl.pallas_call(
        functools.partial(_gdn_kernel, chunk=GDN_CHUNK, n_chunks=n_chunks),
        grid=(b, t // tc),
        in_specs=[smem, smem,
                  tok_block(0), tok_block(1), tok_block(2),
                  tok_block(0),
                  pl.BlockSpec((1, tc, LANES), lambda bi, ti: (bi, ti, 0)),
                  conv_block(0), conv_block(1), conv_block(2),
                  pl.BlockSpec((1, GDN_DK), lambda bi, ti: (0, 0))],
        out_specs=tok_block(0),
        out_shape=jax.ShapeDtypeStruct((b, t, GDN_W), F32),
        scratch_shapes=[pltpu.VMEM((hh, GDN_DK, GDN_DK), F32),
                        pltpu.VMEM((3, SUBLANES, GDN_W), F32),
                        pltpu.VMEM((3, SUBLANES + tc, GDN_W), F32),
                        pltpu.VMEM((LANES, tc), F32)],
        compiler_params=_cparams(("parallel", "arbitrary")),
        name="gdn_core",
    )(a_log, dt_bias, qkv, qkv, qkv, gate, small, conv_w, conv_w, conv_w, onorm_g.reshape(1, GDN_DK))


def _mix_out_kernel(x_ref, o_ref, qm_ref, kv_ref, wo1_ref, wo2_ref, g_ref, b_ref, out_ref, *, o_transposed):
    qm = qm_ref[0]
    kv = kv_ref[0]
    k_mem, v_mem = kv[:, :MEM_W], kv[:, MEM_W:]
    head_of_lane = lax.broadcasted_iota(I32, (1, MEM_W), 1) >> (MEM_DH.bit_length() - 1)
    o_mem = jnp.zeros(qm.shape, F32)
    for hd in range(MEM_HEADS):
        mine = head_of_lane == hd
        s = _mm_nt(jnp.where(mine, qm, 0.0), k_mem) * (MEM_DH ** -0.5)
        s = s - jnp.max(s, axis=-1, keepdims=True)
        p = jnp.exp(s)
        p = p / jnp.sum(p, axis=-1, keepdims=True)
        o_mem = o_mem + jnp.where(mine, _mm(p, v_mem), 0.0)
    y_mix = _mm_tn(o_ref[...], wo1_ref[...]) if o_transposed else _mm(o_ref[0], wo1_ref[...])
    y = y_mix + _mm(o_mem, wo2_ref[...])
    out_ref[0] = _layer_norm(ALPHA * x_ref[0] + y, g_ref[...], b_ref[...])


def _mix_out(x3, o, qm3, kv3, w_o, g, b, *, o_transposed):
    bsz, t, d = x3.shape
    w1 = o.shape[0] if o_transposed else o.shape[-1]
    bm = min(ROW_TILE, t)
    nt = t // bm
    wo1 = w_o[:w1].astype(BF16)
    wo2 = w_o[w1:].astype(BF16)

    def rows(n):
        return pl.BlockSpec((1, bm, n), lambda bi, ti: (bi, ti, 0))

    o_spec = pl.BlockSpec((w1, bm), lambda bi, ti: (0, bi * nt + ti)) if o_transposed else rows(w1)
    return pl.pallas_call(
        functools.partial(_mix_out_kernel, o_transposed=o_transposed),
        grid=(bsz, nt),
        in_specs=[rows(d), o_spec, rows(MEM_W),
                  pl.BlockSpec((1,) + kv3.shape[1:], lambda bi, ti: (bi, 0, 0)),
                  _resident(wo1.shape), _resident(wo2.shape), _resident((1, d)), _resident((1, d))],
        out_specs=rows(d),
        out_shape=jax.ShapeDtypeStruct((bsz, t, d), F32),
        compiler_params=_cparams(("parallel", "parallel")),
        name="mix_out",
    )(x3, o, qm3, kv3, wo1, wo2, g.reshape(1, d), b.reshape(1, d))


def _dsa_proj_kernel(x_ref, w_ref, kvg_ref, lng_ref, lnb_ref,
                     qt_ref, ckv_ref, ckvt_ref, qidxt_ref, kidx_ref, wsmt_ref, qm_ref, *, tk):
    xb = x_ref[...].astype(BF16)
    bm = xb.shape[0]

    def cols(lo, n):
        return jnp.dot(xb, w_ref[:, lo:lo + n], preferred_element_type=F32)

    o1 = DSA_W
    o2 = o1 + DSA_KV_RANK
    o3 = o2 + IDX_HEADS * IDX_DIM
    qt_ref[...] = cols(0, o1).T.astype(qt_ref.dtype)
    c = cols(o1, DSA_KV_RANK)
    c = c * lax.rsqrt(jnp.mean(c * c, axis=-1, keepdims=True) + LN_EPS) * kvg_ref[...]
    ckv_ref[...] = c.astype(ckv_ref.dtype)
    c_t = c.T.astype(ckvt_ref.dtype)
    for j in range(bm // tk):
        ckvt_ref[j, 0:DSA_KV_RANK, :] = c_t[:, j * tk:(j + 1) * tk]
        ckvt_ref[j, DSA_KV_RANK:, :] = jnp.ones((SUM_ROWS, tk), ckvt_ref.dtype)
    qidxt_ref[...] = cols(o2, IDX_HEADS * IDX_DIM).T.astype(qidxt_ref.dtype)
    lane = lax.broadcasted_iota(I32, (1, LANES), 1)
    for half in range(2):
        kx = cols(o3 + half * LANES, LANES)
        live = (lane >= half * IDX_DIM) & (lane < (half + 1) * IDX_DIM)
        mu = jnp.sum(kx, axis=-1, keepdims=True) * (1.0 / IDX_DIM)
        d = jnp.where(live, kx - mu, 0.0)
        var = jnp.sum(d * d, axis=-1, keepdims=True) * (1.0 / IDX_DIM)
        kn = d * lax.rsqrt(var + LN_EPS) * lng_ref[:, half * LANES:(half + 1) * LANES] \
            + lnb_ref[:, half * LANES:(half + 1) * LANES]
        kidx_ref[:, half * LANES:(half + 1) * LANES] = jnp.where(live, kn, 0.0).astype(kidx_ref.dtype)
    w_t = (cols(o3 + 2 * LANES, LANES) * (IDX_HEADS ** -0.5 * IDX_DIM ** -0.5)).T
    wsmt_ref[...] = w_t[:IDX_HEADS, :]
    qm_ref[...] = cols(o3 + 3 * LANES, MEM_W)


def _dsa_proj(x2, w_in, kv_norm_g, kidx_g, kidx_b, tk):
    m, d = x2.shape
    o1 = DSA_W
    o2 = o1 + DSA_KV_RANK
    o3 = o2 + IDX_HEADS * IDX_DIM
    o4 = o3 + IDX_DIM
    o5 = o4 + IDX_HEADS
    zeros64 = jnp.zeros((d, IDX_DIM), w_in.dtype)
    w = jnp.concatenate([
        w_in[:, :o3],
        w_in[:, o3:o4], zeros64, zeros64, w_in[:, o3:o4],
        _pad_cols(w_in[:, o4:o5], LANES),
        w_in[:, o5:],
    ], axis=1).astype(BF16)
    z64 = jnp.zeros((IDX_DIM,), F32)
    lng = jnp.concatenate([kidx_g, z64, z64, kidx_g]).reshape(1, 2 * LANES)
    lnb = jnp.concatenate([kidx_b, z64, z64, kidx_b]).reshape(1, 2 * LANES)
    bm = min(ROW_TILE, m)
    nqi = IDX_HEADS * IDX_DIM
    rr = DSA_KV_RANK

    def tok_major(n):
        return pl.BlockSpec((bm, n), lambda i: (i, 0))

    def feat_major(n):
        return pl.BlockSpec((n, bm), lambda i: (0, i))

    return pl.pallas_call(
        functools.partial(_dsa_proj_kernel, tk=tk),
        grid=(m // bm,),
        in_specs=[tok_major(d), _resident(w.shape),
                  _resident((1, rr)), _resident((1, 2 * LANES)), _resident((1, 2 * LANES))],
        out_specs=[feat_major(DSA_W), tok_major(rr),
                   pl.BlockSpec((bm // tk, rr + SUM_ROWS, tk), lambda i: (i, 0, 0)),
                   feat_major(nqi), tok_major(2 * LANES), feat_major(IDX_HEADS), tok_major(MEM_W)],
        out_shape=[jax.ShapeDtypeStruct((DSA_W, m), BF16),
                   jax.ShapeDtypeStruct((m, rr), BF16),
                   jax.ShapeDtypeStruct((m // tk, rr + SUM_ROWS, tk), BF16),
                   jax.ShapeDtypeStruct((nqi, m), BF16),
                   jax.ShapeDtypeStruct((m, 2 * LANES), BF16),
                   jax.ShapeDtypeStruct((IDX_HEADS, m), F32),
                   jax.ShapeDtypeStruct((m, MEM_W), F32)],
        compiler_params=_cparams(("parallel",)),
        name="dsa_proj",
    )(x2, w, kv_norm_g.reshape(1, rr), lng, lnb)


def _ordered_key_to_f32(u):
    key = u ^ jnp.int32(-2 ** 31)
    bits = key ^ ((key >> 31) & jnp.int32(0x7FFFFFFF))
    return lax.bitcast_convert_type(bits, F32)


def _dsa_attn_kernel(qt_ref, qidxt_ref, wsmt_ref, kidx_ref, ckv_ref, ckvt_ref, wukt_ref, wuvt_ref, o_ref,
                     sc_ref, b0_ref, acc_ref, pos_ref, *, tq, tk, topk, pos_bits):
    hh, rr, dh = DSA_HEADS, DSA_KV_RANK, DSA_DH
    t0 = pl.program_id(1) * tq
    nk = (t0 + tq + tk - 1) // tk
    key_i = lax.broadcasted_iota(I32, (tk, tq), 0)
    t_pos = t0 + lax.broadcasted_iota(I32, (tk, tq), 1)
    neg_inf = jnp.float32(-jnp.inf)

    pairs = IDX_HEADS // 2
    lhs_t = jnp.concatenate([qidxt_ref[p * LANES:(p + 1) * LANES, :] for p in range(pairs)], axis=1)
    w_t = wsmt_ref[...]
    w_rows = [jnp.concatenate([w_t[2 * p + half:2 * p + half + 1, :] for p in range(pairs)], axis=1)
              for half in range(2)]

    def idx_body(kb, carry):
        k0 = pl.multiple_of(kb * tk, tk)
        kblk = kidx_ref[0, pl.ds(k0, tk), :]
        acc = jnp.zeros((tk, tq), F32)
        for half in range(2):
            z = jnp.dot(kblk[:, half * LANES:(half + 1) * LANES], lhs_t, preferred_element_type=F32)
            z = jnp.maximum(z, 0.0) * w_rows[half]
            for p in range(pairs):
                acc = acc + z[:, p * tq:(p + 1) * tq]
        sc_ref[kb] = jnp.where(k0 + key_i <= t_pos, acc, neg_inf)
        return carry

    lax.fori_loop(0, nk, idx_body, 0)

    k_row = jnp.minimum(topk, t0 + lax.broadcasted_iota(I32, (1, tq), 1) + 1).astype(F32)

    def count(pred):
        def body(kb, acc):
            hit = jnp.where(pred(sc_ref[kb], kb), jnp.float32(1.0), jnp.float32(0.0))
            return acc + jnp.sum(hit.reshape(tk // COUNT_ROWS, COUNT_ROWS, tq), axis=0)
        acc = lax.fori_loop(0, nk, body, jnp.zeros((COUNT_ROWS, tq), F32))
        return jnp.sum(acc, axis=0, keepdims=True)

    def radix_cond(c):
        it, _, _, n_open = c
        return (it < 32) & (n_open > 0.0)

    def radix_body(c):
        it, tau_u, done, _ = c
        cand_u = tau_u | jnp.left_shift(jnp.int32(1), 31 - it)
        cand = _ordered_key_to_f32(cand_u)
        cnt = count(lambda blk, kb: blk >= cand)
        tau_u = jnp.where((done == 0.0) & (cnt >= k_row), cand_u, tau_u)
        done = jnp.where(cnt == k_row, 1.0, done)
        return it + 1, tau_u, done, jnp.sum(1.0 - done)

    _, tau_u, _, n_open = lax.while_loop(
        radix_cond, radix_body,
        (jnp.int32(0), jnp.zeros((1, tq), I32), jnp.zeros((1, tq), F32), jnp.float32(tq)))
    tau = _ordered_key_to_f32(tau_u)

    pos_ref[...] = jnp.full((1, tq), NO_INDEX_BOUND, I32)

    @pl.when(n_open > 0.0)
    def _():
        need = k_row - count(lambda blk, kb: blk > tau)

        def pos_body(it, pos):
            cand = pos | jnp.left_shift(jnp.int32(1), pos_bits - 1 - it)
            cnt = count(lambda blk, kb: (blk == tau) & (kb * tk + key_i < cand))
            return jnp.where(cnt < need, cand, pos)

        pos_ref[...] = lax.fori_loop(0, pos_bits, pos_body, jnp.zeros((1, tq), I32))

    pos = pos_ref[...]

    qt = qt_ref[...]
    q_lat_t = jnp.concatenate(
        [jnp.dot(wukt_ref[hd], qt[hd * dh:(hd + 1) * dh, :], preferred_element_type=F32)
         for hd in range(hh)], axis=1)
    q_lat_t = (q_lat_t * (dh ** -0.5 * LOG2E)).astype(BF16)
    slopes = [2.0 ** (-8.0 * (hd + 1) / hh) * LOG2E for hd in range(hh)]
    slope_row = jnp.concatenate([jnp.full((1, tq), sl, F32) for sl in slopes], axis=1)
    key_f = key_i.astype(F32)
    for hd in range(hh):
        b0_ref[:, hd * tq:(hd + 1) * tq] = key_f * slopes[hd]
    acc_ref[...] = jnp.zeros(acc_ref.shape, F32)

    def att_body(kb, m_old):
        k0 = pl.multiple_of(kb * tk, tk)
        s = jnp.dot(ckv_ref[0, pl.ds(k0, tk), :], q_lat_t, preferred_element_type=F32)
        blk = sc_ref[kb]
        sel = (blk > tau) | ((blk == tau) & (k0 + key_i <= pos))
        bias = jnp.where(sel, 0.0, neg_inf)
        x = s + b0_ref[...] + jnp.concatenate([bias] * hh, axis=1)
        r = slope_row * (k0 - t0).astype(F32)
        m_new = jnp.maximum(m_old, jnp.max(x, axis=0, keepdims=True) + r)
        m_safe = jnp.where(m_new == neg_inf, 0.0, m_new)
        p = jnp.exp2(x - (m_safe - r))
        acc_ref[...] = jnp.exp2(m_old - m_safe) * acc_ref[...] + jnp.dot(
            ckvt_ref[0, kb], p.astype(BF16), preferred_element_type=F32)
        return m_new

    lax.fori_loop(0, nk, att_body, jnp.full((1, hh * tq), neg_inf, F32))
    o_lat_t = (acc_ref[0:rr, :] / acc_ref[rr:rr + 1, :]).astype(BF16)
    for hd in range(hh):
        o_ref[hd * dh:(hd + 1) * dh, :] = jnp.dot(
            wuvt_ref[hd], o_lat_t[:, hd * tq:(hd + 1) * tq], preferred_element_type=F32).astype(o_ref.dtype)


def _dsa_attn(b, t, qt, qidxt, wsmt, kidx3, ckv3, ckvt4, w_uk, w_uv, tk):
    tq = min(DSA_TQ, t)
    nq = t // tq
    topk = min(TOPK_MAX, t // 4)
    hh, rr = DSA_HEADS, DSA_KV_RANK
    wukt = w_uk.transpose(0, 2, 1).astype(BF16)
    wuvt = w_uv.transpose(0, 2, 1).astype(BF16)

    def qcols(n):
        return pl.BlockSpec((n, tq), lambda bi, ti: (0, bi * nq + ti))

    return pl.pallas_call(
        functools.partial(_dsa_attn_kernel, tq=tq, tk=tk, topk=topk,
                          pos_bits=max(1, (t - 1).bit_length())),
        grid=(b, nq),
        in_specs=[qcols(DSA_W), qcols(IDX_HEADS * IDX_DIM), qcols(IDX_HEADS),
                  pl.BlockSpec((1, t, 2 * LANES), lambda bi, ti: (bi, 0, 0)),
                  pl.BlockSpec((1, t, rr), lambda bi, ti: (bi, 0, 0)),
                  pl.BlockSpec((1, t // tk, rr + SUM_ROWS, tk), lambda bi, ti: (bi, 0, 0, 0)),
                  _resident(wukt.shape), _resident(wuvt.shape)],
        out_specs=qcols(DSA_W),
        out_shape=jax.ShapeDtypeStruct((DSA_W, b * t), BF16),
        scratch_shapes=[pltpu.VMEM((t // tk, tk, tq), F32),
                        pltpu.VMEM((tk, hh * tq), F32),
                        pltpu.VMEM((rr + SUM_ROWS, hh * tq), F32),
                        pltpu.VMEM((1, tq), I32)],
        compiler_params=_cparams(("parallel", "arbitrary")),
        name="dsa_attn",
    )(qt, qidxt, wsmt, kidx3, ckv3, ckvt4, wukt, wuvt)


def _mem_kv(mem, w_mem_kv):
    b, mlen, d = mem.shape
    (kv,) = _proj(mem.reshape(b * mlen, d), w_mem_kv.astype(BF16), (2 * MEM_W,), (BF16,))
    return kv.reshape(b, mlen, 2 * MEM_W)


def _gdn_layer(x3, mem, w_in, conv_w, a_log, dt_bias, onorm_g, w_mem_kv, w_o, g, b):
    bsz, t, d = x3.shape
    n_qkv = 3 * GDN_W
    n_gate = 4 * GDN_W
    n_small = n_gate + 2 * GDN_HEADS
    w = jnp.concatenate([w_in[:, :n_gate], _pad_cols(w_in[:, n_gate:n_small], LANES),
                         w_in[:, n_small:]], axis=1).astype(BF16)
    qkv, gate, small, q_mem = _proj(x3.reshape(bsz * t, d), w, (n_qkv, GDN_W, LANES, MEM_W),
                                    (F32, F32, F32, F32))
    o = _gdn_core(qkv.reshape(bsz, t, n_qkv), gate.reshape(bsz, t, GDN_W), small.reshape(bsz, t, LANES),
                  conv_w, a_log, dt_bias, onorm_g)
    return _mix_out(x3, o, q_mem.reshape(bsz, t, MEM_W), _mem_kv(mem, w_mem_kv), w_o, g, b,
                    o_transposed=False)


def _dsa_layer(x3, mem, w_in, kv_norm_g, w_uk, w_uv, kidx_g, kidx_b, w_mem_kv, w_o, g, b):
    bsz, t, d = x3.shape
    tk = min(DSA_TK, t)
    qt, ckv, ckvt, qidxt, kidx, wsmt, q_mem = _dsa_proj(x3.reshape(bsz * t, d), w_in, kv_norm_g,
                                                       kidx_g, kidx_b, tk)
    o_t = _dsa_attn(bsz, t, qt, qidxt, wsmt,
                    kidx.reshape(bsz, t, 2 * LANES), ckv.reshape(bsz, t, DSA_KV_RANK),
                    ckvt.reshape(bsz, t // tk, DSA_KV_RANK + SUM_ROWS, tk), w_uk, w_uv, tk)
    return _mix_out(x3, o_t, q_mem.reshape(bsz, t, MEM_W), _mem_kv(mem, w_mem_kv), w_o, g, b,
                    o_transposed=True)


def kernel(x, mem, l0_ffn1_w_in, l0_ffn1_w_out, l0_ln1_g, l0_ln1_b, l0_w_in, l0_conv_w, l0_a_log, l0_dt_bias, l0_onorm_g, l0_w_mem_kv, l0_w_o, l0_ln2_g, l0_ln2_b, l0_ffn2_w_in, l0_ffn2_w_out, l0_ln3_g, l0_ln3_b, l1_ffn1_w_in, l1_ffn1_w_out, l1_ln1_g, l1_ln1_b, l1_w_in, l1_kv_norm_g, l1_w_uk, l1_w_uv, l1_kidx_ln_g, l1_kidx_ln_b, l1_w_mem_kv, l1_w_o, l1_ln2_g, l1_ln2_b, l1_ffn2_w_in, l1_ffn2_w_out, l1_ln3_g, l1_ln3_b):
    bsz, t, d = x.shape

    def ffn(h3, w_in, w_out, g, b):
        return _ffn_ln(h3.reshape(bsz * t, d), w_in, w_out, g, b).reshape(bsz, t, d)

    h = ffn(x, l0_ffn1_w_in, l0_ffn1_w_out, l0_ln1_g, l0_ln1_b)
    h = _gdn_layer(h, mem, l0_w_in, l0_conv_w, l0_a_log, l0_dt_bias, l0_onorm_g, l0_w_mem_kv, l0_w_o,
                   l0_ln2_g, l0_ln2_b)
    h = ffn(h, l0_ffn2_w_in, l0_ffn2_w_out, l0_ln3_g, l0_ln3_b)
    h = ffn(h, l1_ffn1_w_in, l1_ffn1_w_out, l1_ln1_g, l1_ln1_b)
    h = _dsa_layer(h, mem, l1_w_in, l1_kv_norm_g, l1_w_uk, l1_w_uv, l1_kidx_ln_g, l1_kidx_ln_b,
                   l1_w_mem_kv, l1_w_o, l1_ln2_g, l1_ln2_b)
    h = ffn(h, l1_ffn2_w_in, l1_ffn2_w_out, l1_ln3_g, l1_ln3_b)
    return h
```

```python
import functools
import math

import jax
import jax.numpy as jnp
from jax import lax
from jax.experimental import pallas as pl
from jax.experimental.pallas import tpu as pltpu

F32 = jnp.float32
BF16 = jnp.bfloat16
I32 = jnp.int32

DEPTH = 2
LN_EPS = 1e-5
ALPHA = (2 * DEPTH) ** 0.25
GDN_HEADS = 6
GDN_DK = 128
GDN_W = GDN_HEADS * GDN_DK
GDN_CONV = 4
DSA_HEADS = 12
DSA_DH = 64
DSA_W = DSA_HEADS * DSA_DH
DSA_KV_RANK = 256
IDX_HEADS = 8
IDX_DIM = 64
TOPK_MAX = 256
MEM_HEADS = 4
MEM_DH = 64
MEM_W = MEM_HEADS * MEM_DH

LANES = 128
SUBLANES = 8
VMEM_LIMIT_BYTES = 56 * 1024 * 1024

ROW_TILE = 512
FFN_ROW_TILE = 1024
FFN_CHUNK = 256
GDN_CHUNK = 128
GDN_TILE = 256
INV_BLOCK = 16
DSA_TQ = LANES
DSA_TK = 512
NO_INDEX_BOUND = 2 ** 30
COUNT_ROWS = 8 * SUBLANES
SUM_ROWS = 2 * SUBLANES
RADIX_FIXED = 16
RADIX_CHECK = 4
LOG2E = math.log2(math.e)


def _cparams(sem):
    return pltpu.CompilerParams(dimension_semantics=sem, vmem_limit_bytes=VMEM_LIMIT_BYTES)


def _resident(shape):
    nd = len(shape)
    return pl.BlockSpec(shape, lambda *_: (0,) * nd, pipeline_mode=pl.Buffered(1))


def _mm(a, b):
    return jnp.dot(a.astype(BF16), b.astype(BF16), preferred_element_type=F32)


def _mm_nt(a, b):
    return lax.dot_general(a.astype(BF16), b.astype(BF16), (((1,), (1,)), ((), ())),
                           preferred_element_type=F32)


def _mm_tn(a, b):
    return lax.dot_general(a.astype(BF16), b.astype(BF16), (((0,), (0,)), ((), ())),
                           preferred_element_type=F32)


def _layer_norm(y, g, b):
    mu = jnp.mean(y, axis=-1, keepdims=True)
    d = y - mu
    var = jnp.mean(d * d, axis=-1, keepdims=True)
    return d * lax.rsqrt(var + LN_EPS) * g + b


def _silu(x):
    return x * jax.nn.sigmoid(x)


def _ffn_kernel(x_ref, wgu_ref, wo_ref, g_ref, b_ref, o_ref, acc_ref, *, n_chunks, fc):
    x = x_ref[...]
    xb = x.astype(BF16)
    for c in range(n_chunks):
        gu = jnp.dot(xb, wgu_ref[c], preferred_element_type=F32)
        h = (_silu(gu[:, :fc]) * gu[:, fc:]).astype(BF16)
        part = jnp.dot(h, wo_ref[c], preferred_element_type=F32)
        if c == 0:
            acc_ref[...] = part
        else:
            acc_ref[...] += part
    o_ref[...] = _layer_norm(ALPHA * x + 0.5 * acc_ref[...], g_ref[...], b_ref[...])


def _prep_ffn(w_in, w_out):
    d, two_f = w_in.shape
    dff = two_f // 2
    n = -(-dff // FFN_CHUNK)
    pad = n * FFN_CHUNK - dff
    wgu = jnp.pad(w_in.astype(BF16).reshape(d, 2, dff), ((0, 0), (0, 0), (0, pad)))
    wgu = wgu.reshape(d, 2, n, FFN_CHUNK).transpose(2, 0, 1, 3).reshape(n, d, 2 * FFN_CHUNK)
    wo = jnp.pad(w_out, ((0, pad), (0, 0))).reshape(n, FFN_CHUNK, d).astype(BF16)
    return wgu, wo


def _ffn_ln(x2, w_in, w_out, g, b):
    m, d = x2.shape
    wgu, wo = _prep_ffn(w_in, w_out)
    n = wgu.shape[0]
    bm = min(FFN_ROW_TILE, m)
    row = pl.BlockSpec((bm, d), lambda i: (i, 0))
    return pl.pallas_call(
        functools.partial(_ffn_kernel, n_chunks=n, fc=FFN_CHUNK),
        grid=(m // bm,),
        in_specs=[row, _resident(wgu.shape), _resident(wo.shape), _resident((1, d)), _resident((1, d))],
        out_specs=row,
        out_shape=jax.ShapeDtypeStruct((m, d), F32),
        scratch_shapes=[pltpu.VMEM((bm, d), F32)],
        compiler_params=_cparams(("parallel",)),
        name="ffn_ln",
    )(x2, wgu, wo, g.reshape(1, d), b.reshape(1, d))


def _proj_kernel(x_ref, w_ref, *o_refs, widths):
    xb = x_ref[...].astype(BF16)
    off = 0
    for o_ref, n in zip(o_refs, widths):
        o_ref[...] = jnp.dot(xb, w_ref[:, off:off + n], preferred_element_type=F32).astype(o_ref.dtype)
        off += n


def _proj(x2, w, widths, dtypes):
    m, d = x2.shape
    bm = min(ROW_TILE, m)
    return pl.pallas_call(
        functools.partial(_proj_kernel, widths=widths),
        grid=(m // bm,),
        in_specs=[pl.BlockSpec((bm, d), lambda i: (i, 0)), _resident(w.shape)],
        out_specs=[pl.BlockSpec((bm, n), lambda i: (i, 0)) for n in widths],
        out_shape=[jax.ShapeDtypeStruct((m, n), dt) for n, dt in zip(widths, dtypes)],
        compiler_params=_cparams(("parallel",)),
        name="proj",
    )(x2, w)


def _pad_cols(w, n):
    return jnp.pad(w, ((0, 0), (0, n - w.shape[1])))


def _unit_lower_inverse(mats, xor_idx, eye):
    c = mats[0].shape[0]
    shift = int(math.log2(INV_BLOCK))
    x = [jnp.where((xor_idx >> shift) == 0, a, 0.0) for a in mats]
    p = [eye - xi for xi in x]
    for _ in range(shift - 1):
        x = [_mm(xi, xi) for xi in x]
        p = [pi + _mm(pi, xi) for pi, xi in zip(p, x)]
    t = p
    while (1 << shift) < c:
        off = [jnp.where((xor_idx >> shift) == 1, a, 0.0) for a in mats]
        t_off = [_mm(ti, oi) for ti, oi in zip(t, off)]
        t = [ti - _mm(toi, ti) for ti, toi in zip(t, t_off)]
        shift += 1
    return t


def _gdn_kernel(alog_ref, dtb_ref, q_ref, k_ref, v_ref, gate_ref, small_ref,
                cwq_ref, cwk_ref, cwv_ref, og_ref, o_ref,
                s_ref, halo_ref, stage_ref, smt_ref, *, chunk, n_chunks):
    tc = chunk * n_chunks

    @pl.when(pl.program_id(1) == 0)
    def _():
        s_ref[...] = jnp.zeros_like(s_ref)
        halo_ref[...] = jnp.zeros_like(halo_ref)

    def conv_silu(a, raw_ref, cw_ref):
        raw = raw_ref[0]
        stage_ref[a, 0:SUBLANES, :] = halo_ref[a]
        stage_ref[a, SUBLANES:SUBLANES + tc, :] = raw
        halo_ref[a] = raw[tc - SUBLANES:tc, :]
        cw = cw_ref[...]
        y = cw[GDN_CONV - 1:GDN_CONV, :] * raw
        for j in range(GDN_CONV - 1):
            lo = SUBLANES - (GDN_CONV - 1) + j
            y = y + cw[j:j + 1, :] * stage_ref[a, lo:lo + tc, :]
        return _silu(y)

    q_all = conv_silu(0, q_ref, cwq_ref)
    k_all = conv_silu(1, k_ref, cwk_ref)
    v_all = conv_silu(2, v_ref, cwv_ref)

    smt_ref[...] = small_ref[0].T
    lane_in_chunk = lax.broadcasted_iota(I32, (SUBLANES, tc), 1) & (chunk - 1)

    def to_cols(row):
        return jnp.broadcast_to(row[0:1, :], (LANES, tc)).T

    ri = lax.broadcasted_iota(I32, (chunk, chunk), 0)
    ci = lax.broadcasted_iota(I32, (chunk, chunk), 1)
    incl = ri >= ci
    strict = ri > ci
    xor_idx = ri ^ ci
    eye = jnp.where(ri == ci, 1.0, 0.0).astype(F32)

    def each(f, *lists):
        return [f(*args) for args in zip(*lists)]

    heads = list(range(GDN_HEADS))
    col_of = [slice(h * GDN_DK, (h + 1) * GDN_DK) for h in heads]
    q_h = [q_all[:, cs] for cs in col_of]
    k_h = [k_all[:, cs] for cs in col_of]
    v_h = [v_all[:, cs] for cs in col_of]
    q_h = each(lambda q: q * lax.rsqrt(jnp.sum(q * q, axis=-1, keepdims=True) + 1e-6) * (GDN_DK ** -0.5), q_h)
    k_h = each(lambda k: k * lax.rsqrt(jnp.sum(k * k, axis=-1, keepdims=True) + 1e-6), k_h)
    beta_row = [jax.nn.sigmoid(jnp.broadcast_to(smt_ref[h:h + 1, :], (SUBLANES, tc))) for h in heads]
    g_row = [-jnp.exp(jnp.full((SUBLANES, tc), alog_ref[h], F32))
             * jax.nn.softplus(jnp.broadcast_to(smt_ref[GDN_HEADS + h:GDN_HEADS + h + 1, :], (SUBLANES, tc))
                               + dtb_ref[h]) for h in heads]
    gc_row = g_row
    step = 1
    while step < chunk:
        gc_row = each(lambda g: g + jnp.where(lane_in_chunk >= step, pltpu.roll(g, step, axis=1), 0.0), gc_row)
        step *= 2
    beta_c = each(to_cols, beta_row)
    gc_c = each(to_cols, gc_row)

    probs = [(c, h) for c in range(n_chunks) for h in heads]
    rows_of = [slice(c * chunk, (c + 1) * chunk) for c, _ in probs]
    qc = [q_h[h][rows_of[i]] for i, (_, h) in enumerate(probs)]
    kc = [k_h[h][rows_of[i]] for i, (_, h) in enumerate(probs)]
    vc = [v_h[h][rows_of[i]] for i, (_, h) in enumerate(probs)]
    bc = [beta_c[h][rows_of[i]] for i, (_, h) in enumerate(probs)]
    gcc = [gc_c[h][rows_of[i]] for i, (_, h) in enumerate(probs)]
    gcr = [gc_row[h][0:1, rows_of[i]] for i, (_, h) in enumerate(probs)]
    decay = each(lambda a, b: jnp.where(incl, jnp.exp(jnp.where(incl, a - b, 0.0)), 0.0), gcc, gcr)
    kb = each(lambda a, b: a * b, kc, bc)
    a_mat = each(lambda a, b, d: jnp.where(strict, _mm_nt(a, b) * d, 0.0), kb, kc, decay)
    t_mat = _unit_lower_inverse(a_mat, xor_idx, eye)
    eg = each(jnp.exp, gcc)
    uw = each(lambda t, v, b, kbi, e: _mm(t, jnp.concatenate([v * b, kbi * e], axis=1)), t_mat, vc, bc, kb, eg)
    qk = each(lambda a, b, d: jnp.where(incl, _mm_nt(a, b) * d, 0.0), qc, kc, decay)
    g_last = each(lambda g: g[chunk - 1:chunk, :], gcc)
    k_dec = each(lambda k, gl, g: k * jnp.exp(gl - g), kc, g_last, gcc)
    lhs = each(lambda w, q, e: jnp.concatenate([w[:, GDN_DK:], q * e], axis=0), uw, qc, eg)

    state = [s_ref[h] for h in heads]
    for c in range(n_chunks):
        idx = [c * GDN_HEADS + h for h in heads]
        ws = [_mm(lhs[i], state[h]) for i, h in zip(idx, heads)]
        u = [uw[i][:, :GDN_DK] - w[:chunk] for i, w in zip(idx, ws)]
        o = [w[chunk:] + _mm(qk[i], ui) for i, w, ui in zip(idx, ws, u)]
        state = [state[h] * jnp.exp(g_last[i]) + _mm_tn(k_dec[i], ui) for i, h, ui in zip(idx, heads, u)]
        o = each(lambda x: x * lax.rsqrt(jnp.mean(x * x, axis=-1, keepdims=True) + LN_EPS) * og_ref[...], o)
        for h in heads:
            o_ref[0, rows_of[idx[h]], col_of[h]] = o[h] * _silu(gate_ref[0, rows_of[idx[h]], col_of[h]])
    for h in heads:
        s_ref[h] = state[h]


def _gdn_core(qkv, gate, small, conv_w, a_log, dt_bias, onorm_g):
    b, t, _ = qkv.shape
    tc = min(GDN_TILE, t)
    n_chunks = tc // GDN_CHUNK
    hh = GDN_HEADS

    def tok_block(j):
        return pl.BlockSpec((1, tc, GDN_W), lambda bi, ti: (bi, ti, j))

    def conv_block(j):
        return pl.BlockSpec((GDN_CONV, GDN_W), lambda bi, ti: (0, j))

    smem = pl.BlockSpec(memory_space=pltpu.SMEM)
    return pl.pallas_call(
        functools.partial(_gdn_kernel, chunk=GDN_CHUNK, n_chunks=n_chunks),
        grid=(b, t // tc),
        in_specs=[smem, smem,
                  tok_block(0), tok_block(1), tok_block(2),
                  tok_block(0),
                  pl.BlockSpec((1, tc, LANES), lambda bi, ti: (bi, ti, 0)),
                  conv_block(0), conv_block(1), conv_block(2),
                  pl.BlockSpec((1, GDN_DK), lambda bi, ti: (0, 0))],
        out_specs=tok_block(0),
        out_shape=jax.ShapeDtypeStruct((b, t, GDN_W), F32),
        scratch_shapes=[pltpu.VMEM((hh, GDN_DK, GDN_DK), F32),
                        pltpu.VMEM((3, SUBLANES, GDN_W), F32),
                        pltpu.VMEM((3, SUBLANES + tc, GDN_W), F32),
                        pltpu.VMEM((LANES, tc), F32)],
        compiler_params=_cparams(("parallel", "arbitrary")),
        name="gdn_core",
    )(a_log, dt_bias, qkv, qkv, qkv, gate, small, conv_w, conv_w, conv_w, onorm_g.reshape(1, GDN_DK))


def _mix_out_kernel(x_ref, o_ref, qm_ref, kv_ref, wo1_ref, wo2_ref, g_ref, b_ref, out_ref, *, o_transposed):
    qm = qm_ref[0]
    kv = kv_ref[0]
    k_mem, v_mem = kv[:, :MEM_W], kv[:, MEM_W:]
    head_of_lane = lax.broadcasted_iota(I32, (1, MEM_W), 1) >> (MEM_DH.bit_length() - 1)
    o_mem = jnp.zeros(qm.shape, F32)
    for hd in range(MEM_HEADS):
        mine = head_of_lane == hd
        s = _mm_nt(jnp.where(mine, qm, 0.0), k_mem) * (MEM_DH ** -0.5)
        s = s - jnp.max(s, axis=-1, keepdims=True)
        p = jnp.exp(s)
        p = p / jnp.sum(p, axis=-1, keepdims=True)
        o_mem = o_mem + jnp.where(mine, _mm(p, v_mem), 0.0)
    y_mix = _mm_tn(o_ref[...], wo1_ref[...]) if o_transposed else _mm(o_ref[0], wo1_ref[...])
    y = y_mix + _mm(o_mem, wo2_ref[...])
    out_ref[0] = _layer_norm(ALPHA * x_ref[0] + y, g_ref[...], b_ref[...])


def _mix_out(x3, o, qm3, kv3, w_o, g, b, *, o_transposed):
    bsz, t, d = x3.shape
    w1 = o.shape[0] if o_transposed else o.shape[-1]
    bm = min(ROW_TILE, t)
    nt = t // bm
    wo1 = w_o[:w1].astype(BF16)
    wo2 = w_o[w1:].astype(BF16)

    def rows(n):
        return pl.BlockSpec((1, bm, n), lambda bi, ti: (bi, ti, 0))

    o_spec = pl.BlockSpec((w1, bm), lambda bi, ti: (0, bi * nt + ti)) if o_transposed else rows(w1)
    return pl.pallas_call(
        functools.partial(_mix_out_kernel, o_transposed=o_transposed),
        grid=(bsz, nt),
        in_specs=[rows(d), o_spec, rows(MEM_W),
                  pl.BlockSpec((1,) + kv3.shape[1:], lambda bi, ti: (bi, 0, 0)),
                  _resident(wo1.shape), _resident(wo2.shape), _resident((1, d)), _resident((1, d))],
        out_specs=rows(d),
        out_shape=jax.ShapeDtypeStruct((bsz, t, d), F32),
        compiler_params=_cparams(("parallel", "parallel")),
        name="mix_out",
    )(x3, o, qm3, kv3, wo1, wo2, g.reshape(1, d), b.reshape(1, d))


def _dsa_proj_kernel(x_ref, w_ref, kvg_ref, lng_ref, lnb_ref,
                     qt_ref, ckv_ref, ckvt_ref, qidxt_ref, kidx_ref, wsmt_ref, qm_ref, *, tk):
    xb = x_ref[...].astype(BF16)
    bm = xb.shape[0]

    def cols(lo, n):
        return jnp.dot(xb, w_ref[:, lo:lo + n], preferred_element_type=F32)

    o1 = DSA_W
    o2 = o1 + DSA_KV_RANK
    o3 = o2 + IDX_HEADS * IDX_DIM
    qt_ref[...] = cols(0, o1).T.astype(qt_ref.dtype)
    c = cols(o1, DSA_KV_RANK)
    c = c * lax.rsqrt(jnp.mean(c * c, axis=-1, keepdims=True) + LN_EPS) * kvg_ref[...]
    ckv_ref[...] = c.astype(ckv_ref.dtype)
    c_t = c.T.astype(ckvt_ref.dtype)
    for j in range(bm // tk):
        ckvt_ref[j, 0:DSA_KV_RANK, :] = c_t[:, j * tk:(j + 1) * tk]
        ckvt_ref[j, DSA_KV_RANK:, :] = jnp.ones((SUM_ROWS, tk), ckvt_ref.dtype)
    qidxt_ref[...] = cols(o2, IDX_HEADS * IDX_DIM).T.astype(qidxt_ref.dtype)
    lane = lax.broadcasted_iota(I32, (1, LANES), 1)
    for half in range(2):
        kx = cols(o3 + half * LANES, LANES)
        live = (lane >= half * IDX_DIM) & (lane < (half + 1) * IDX_DIM)
        mu = jnp.sum(kx, axis=-1, keepdims=True) * (1.0 / IDX_DIM)
        d = jnp.where(live, kx - mu, 0.0)
        var = jnp.sum(d * d, axis=-1, keepdims=True) * (1.0 / IDX_DIM)
        kn = d * lax.rsqrt(var + LN_EPS) * lng_ref[:, half * LANES:(half + 1) * LANES] \
            + lnb_ref[:, half * LANES:(half + 1) * LANES]
        kidx_ref[:, half * LANES:(half + 1) * LANES] = jnp.where(live, kn, 0.0).astype(kidx_ref.dtype)
    w_t = (cols(o3 + 2 * LANES, LANES) * (IDX_HEADS ** -0.5 * IDX_DIM ** -0.5)).T
    wsmt_ref[...] = w_t[:IDX_HEADS, :]
    qm_ref[...] = cols(o3 + 3 * LANES, MEM_W)


def _dsa_proj(x2, w_in, kv_norm_g, kidx_g, kidx_b, tk):
    m, d = x2.shape
    o1 = DSA_W
    o2 = o1 + DSA_KV_RANK
    o3 = o2 + IDX_HEADS * IDX_DIM
    o4 = o3 + IDX_DIM
    o5 = o4 + IDX_HEADS
    zeros64 = jnp.zeros((d, IDX_DIM), w_in.dtype)
    w = jnp.concatenate([
        w_in[:, :o3],
        w_in[:, o3:o4], zeros64, zeros64, w_in[:, o3:o4],
        _pad_cols(w_in[:, o4:o5], LANES),
        w_in[:, o5:],
    ], axis=1).astype(BF16)
    z64 = jnp.zeros((IDX_DIM,), F32)
    lng = jnp.concatenate([kidx_g, z64, z64, kidx_g]).reshape(1, 2 * LANES)
    lnb = jnp.concatenate([kidx_b, z64, z64, kidx_b]).reshape(1, 2 * LANES)
    bm = min(ROW_TILE, m)
    nqi = IDX_HEADS * IDX_DIM
    rr = DSA_KV_RANK

    def tok_major(n):
        return pl.BlockSpec((bm, n), lambda i: (i, 0))

    def feat_major(n):
        return pl.BlockSpec((n, bm), lambda i: (0, i))

    return pl.pallas_call(
        functools.partial(_dsa_proj_kernel, tk=tk),
        grid=(m // bm,),
        in_specs=[tok_major(d), _resident(w.shape),
                  _resident((1, rr)), _resident((1, 2 * LANES)), _resident((1, 2 * LANES))],
        out_specs=[feat_major(DSA_W), tok_major(rr),
                   pl.BlockSpec((bm // tk, rr + SUM_ROWS, tk), lambda i: (i, 0, 0)),
                   feat_major(nqi), tok_major(2 * LANES), feat_major(IDX_HEADS), tok_major(MEM_W)],
        out_shape=[jax.ShapeDtypeStruct((DSA_W, m), BF16),
                   jax.ShapeDtypeStruct((m, rr), BF16),
                   jax.ShapeDtypeStruct((m // tk, rr + SUM_ROWS, tk), BF16),
                   jax.ShapeDtypeStruct((nqi, m), BF16),
                   jax.ShapeDtypeStruct((m, 2 * LANES), BF16),
                   jax.ShapeDtypeStruct((IDX_HEADS, m), F32),
                   jax.ShapeDtypeStruct((m, MEM_W), F32)],
        compiler_params=_cparams(("parallel",)),
        name="dsa_proj",
    )(x2, w, kv_norm_g.reshape(1, rr), lng, lnb)


def _ordered_key_to_f32(u):
    key = u ^ jnp.int32(-2 ** 31)
    bits = key ^ ((key >> 31) & jnp.int32(0x7FFFFFFF))
    return lax.bitcast_convert_type(bits, F32)


def _dsa_attn_kernel(qt_ref, qidxt_ref, wsmt_ref, kidx_ref, ckv_ref, ckvt_ref, wukt_ref, wuvt_ref, o_ref,
                     sc_ref, b0_ref, acc_ref, pos_ref, *, tq, tk, topk, pos_bits):
    hh, rr, dh = DSA_HEADS, DSA_KV_RANK, DSA_DH
    t0 = pl.program_id(1) * tq
    nk = (t0 + tq + tk - 1) // tk
    key_i = lax.broadcasted_iota(I32, (tk, tq), 0)
    t_pos = t0 + lax.broadcasted_iota(I32, (tk, tq), 1)
    neg_inf = jnp.float32(-jnp.inf)

    pairs = IDX_HEADS // 2
    lhs_t = jnp.concatenate([qidxt_ref[p * LANES:(p + 1) * LANES, :] for p in range(pairs)], axis=1)
    w_t = wsmt_ref[...]
    w_rows = [jnp.concatenate([w_t[2 * p + half:2 * p + half + 1, :] for p in range(pairs)], axis=1)
              for half in range(2)]

    def idx_body(kb, carry):
        k0 = pl.multiple_of(kb * tk, tk)
        kblk = kidx_ref[0, pl.ds(k0, tk), :]
        acc = jnp.zeros((tk, tq), F32)
        for half in range(2):
            z = jnp.dot(kblk[:, half * LANES:(half + 1) * LANES], lhs_t, preferred_element_type=F32)
            z = jnp.maximum(z, 0.0) * w_rows[half]
            for p in range(pairs):
                acc = acc + z[:, p * tq:(p + 1) * tq]
        sc_ref[kb] = jnp.where(k0 + key_i <= t_pos, acc, neg_inf)
        return carry

    lax.fori_loop(0, nk, idx_body, 0)

    n_valid = (t0 + lax.broadcasted_iota(I32, (1, tq), 1) + 1).astype(F32)
    k_row = jnp.minimum(jnp.float32(topk), n_valid)

    def count(pred):
        def body(kb, acc):
            hit = jnp.where(pred(sc_ref[kb], kb), jnp.float32(1.0), jnp.float32(0.0))
            return acc + jnp.sum(hit.reshape(tk // COUNT_ROWS, COUNT_ROWS, tq), axis=0)
        acc = lax.fori_loop(0, nk, body, jnp.zeros((COUNT_ROWS, tq), F32))
        return jnp.sum(acc, axis=0, keepdims=True)

    def radix_step(it, c):
        tau_u, done = c
        cand_u = tau_u | jnp.left_shift(jnp.int32(1), 31 - it)
        cand = _ordered_key_to_f32(cand_u)
        cnt = count(lambda blk, kb: blk >= cand)
        tau_u = jnp.where((done == 0.0) & (cnt >= k_row), cand_u, tau_u)
        return tau_u, jnp.where(cnt == k_row, 1.0, done)

    def radix_chunk(c):
        it, tau_u, done, _ = c
        tau_u, done = lax.fori_loop(0, RADIX_CHECK, lambda j, s: radix_step(it + j, s), (tau_u, done))
        return it + RADIX_CHECK, tau_u, done, jnp.sum(1.0 - done)

    tau_u, done = lax.fori_loop(0, RADIX_FIXED, radix_step,
                                (jnp.zeros((1, tq), I32), jnp.zeros((1, tq), F32)))
    _, tau_u, _, n_open = lax.while_loop(
        lambda c: (c[0] < 32) & (c[3] > 0.0), radix_chunk,
        (jnp.int32(RADIX_FIXED), tau_u, done, jnp.sum(1.0 - done)))
    tau = _ordered_key_to_f32(tau_u)

    pos_ref[...] = jnp.full((1, tq), NO_INDEX_BOUND, I32)

    @pl.when(n_open > 0.0)
    def _():
        need = k_row - count(lambda blk, kb: blk > tau)

        def pos_body(it, pos):
            cand = pos | jnp.left_shift(jnp.int32(1), pos_bits - 1 - it)
            cnt = count(lambda blk, kb: (blk == tau) & (kb * tk + key_i < cand))
            return jnp.where(cnt < need, cand, pos)

        pos_ref[...] = lax.fori_loop(0, pos_bits, pos_body, jnp.zeros((1, tq), I32))

    pos = pos_ref[...]

    qt = qt_ref[...]
    q_lat_t = jnp.concatenate(
        [jnp.dot(wukt_ref[hd], qt[hd * dh:(hd + 1) * dh, :], preferred_element_type=F32)
         for hd in range(hh)], axis=1)
    q_lat_t = (q_lat_t * (dh ** -0.5 * LOG2E)).astype(BF16)
    slopes = [2.0 ** (-8.0 * (hd + 1) / hh) * LOG2E for hd in range(hh)]
    slope_row = jnp.concatenate([jnp.full((1, tq), sl, F32) for sl in slopes], axis=1)
    key_f = key_i.astype(F32)
    for hd in range(hh):
        b0_ref[:, hd * tq:(hd + 1) * tq] = key_f * slopes[hd]
    acc_ref[...] = jnp.zeros(acc_ref.shape, F32)

    def att_body(kb, m_old):
        k0 = pl.multiple_of(kb * tk, tk)
        s = jnp.dot(ckv_ref[0, pl.ds(k0, tk), :], q_lat_t, preferred_element_type=F32)
        blk = sc_ref[kb]
        sel = (blk > tau) | ((blk == tau) & (k0 + key_i <= pos))
        bias = jnp.where(sel, 0.0, neg_inf)
        x = s + b0_ref[...] + jnp.concatenate([bias] * hh, axis=1)
        r = slope_row * (k0 - t0).astype(F32)
        m_new = jnp.maximum(m_old, jnp.max(x, axis=0, keepdims=True) + r)
        m_safe = jnp.where(m_new == neg_inf, 0.0, m_new)
        p = jnp.exp2(x - (m_safe - r))
        acc_ref[...] = jnp.exp2(m_old - m_safe) * acc_ref[...] + jnp.dot(
            ckvt_ref[0, kb], p.astype(BF16), preferred_element_type=F32)
        return m_new

    lax.fori_loop(0, nk, att_body, jnp.full((1, hh * tq), neg_inf, F32))
    o_lat_t = (acc_ref[0:rr, :] / acc_ref[rr:rr + 1, :]).astype(BF16)
    for hd in range(hh):
        o_ref[hd * dh:(hd + 1) * dh, :] = jnp.dot(
            wuvt_ref[hd], o_lat_t[:, hd * tq:(hd + 1) * tq], preferred_element_type=F32).astype(o_ref.dtype)


def _dsa_attn(b, t, qt, qidxt, wsmt, kidx3, ckv3, ckvt4, w_uk, w_uv, tk):
    tq = min(DSA_TQ, t)
    nq = t // tq
    topk = min(TOPK_MAX, t // 4)
    hh, rr = DSA_HEADS, DSA_KV_RANK
    wukt = w_uk.transpose(0, 2, 1).astype(BF16)
    wuvt = w_uv.transpose(0, 2, 1).astype(BF16)

    def qcols(n):
        return pl.BlockSpec((n, tq), lambda bi, ti: (0, bi * nq + ti))

    return pl.pallas_call(
        functools.partial(_dsa_attn_kernel, tq=tq, tk=tk, topk=topk,
                          pos_bits=max(1, (t - 1).bit_length())),
        grid=(b, nq),
        in_specs=[qcols(DSA_W), qcols(IDX_HEADS * IDX_DIM), qcols(IDX_HEADS),
                  pl.BlockSpec((1, t, 2 * LANES), lambda bi, ti: (bi, 0, 0)),
                  pl.BlockSpec((1, t, rr), lambda bi, ti: (bi, 0, 0)),
                  pl.BlockSpec((1, t // tk, rr + SUM_ROWS, tk), lambda bi, ti: (bi, 0, 0, 0)),
                  _resident(wukt.shape), _resident(wuvt.shape)],
        out_specs=qcols(DSA_W),
        out_shape=jax.ShapeDtypeStruct((DSA_W, b * t), BF16),
        scratch_shapes=[pltpu.VMEM((t // tk, tk, tq), F32),
                        pltpu.VMEM((tk, hh * tq), F32),
                        pltpu.VMEM((rr + SUM_ROWS, hh * tq), F32),
                        pltpu.VMEM((1, tq), I32)],
        compiler_params=_cparams(("parallel", "arbitrary")),
        name="dsa_attn",
    )(qt, qidxt, wsmt, kidx3, ckv3, ckvt4, wukt, wuvt)


def _mem_kv(mem, w_mem_kv):
    b, mlen, d = mem.shape
    (kv,) = _proj(mem.reshape(b * mlen, d), w_mem_kv.astype(BF16), (2 * MEM_W,), (BF16,))
    return kv.reshape(b, mlen, 2 * MEM_W)


def _gdn_layer(x3, mem, w_in, conv_w, a_log, dt_bias, onorm_g, w_mem_kv, w_o, g, b):
    bsz, t, d = x3.shape
    n_qkv = 3 * GDN_W
    n_gate = 4 * GDN_W
    n_small = n_gate + 2 * GDN_HEADS
    w = jnp.concatenate([w_in[:, :n_gate], _pad_cols(w_in[:, n_gate:n_small], LANES),
                         w_in[:, n_small:]], axis=1).astype(BF16)
    qkv, gate, small, q_mem = _proj(x3.reshape(bsz * t, d), w, (n_qkv, GDN_W, LANES, MEM_W),
                                    (F32, F32, F32, F32))
    o = _gdn_core(qkv.reshape(bsz, t, n_qkv), gate.reshape(bsz, t, GDN_W), small.reshape(bsz, t, LANES),
                  conv_w, a_log, dt_bias, onorm_g)
    return _mix_out(x3, o, q_mem.reshape(bsz, t, MEM_W), _mem_kv(mem, w_mem_kv), w_o, g, b,
                    o_transposed=False)


def _dsa_layer(x3, mem, w_in, kv_norm_g, w_uk, w_uv, kidx_g, kidx_b, w_mem_kv, w_o, g, b):
    bsz, t, d = x3.shape
    tk = min(DSA_TK, t)
    qt, ckv, ckvt, qidxt, kidx, wsmt, q_mem = _dsa_proj(x3.reshape(bsz * t, d), w_in, kv_norm_g,
                                                       kidx_g, kidx_b, tk)
    o_t = _dsa_attn(bsz, t, qt, qidxt, wsmt,
                    kidx.reshape(bsz, t, 2 * LANES), ckv.reshape(bsz, t, DSA_KV_RANK),
                    ckvt.reshape(bsz, t // tk, DSA_KV_RANK + SUM_ROWS, tk), w_uk, w_uv, tk)
    return _mix_out(x3, o_t, q_mem.reshape(bsz, t, MEM_W), _mem_kv(mem, w_mem_kv), w_o, g, b,
                    o_transposed=True)


def kernel(x, mem, l0_ffn1_w_in, l0_ffn1_w_out, l0_ln1_g, l0_ln1_b, l0_w_in, l0_conv_w, l0_a_log, l0_dt_bias, l0_onorm_g, l0_w_mem_kv, l0_w_o, l0_ln2_g, l0_ln2_b, l0_ffn2_w_in, l0_ffn2_w_out, l0_ln3_g, l0_ln3_b, l1_ffn1_w_in, l1_ffn1_w_out, l1_ln1_g, l1_ln1_b, l1_w_in, l1_kv_norm_g, l1_w_uk, l1_w_uv, l1_kidx_ln_g, l1_kidx_ln_b, l1_w_mem_kv, l1_w_o, l1_ln2_g, l1_ln2_b, l1_ffn2_w_in, l1_ffn2_w_out, l1_ln3_g, l1_ln3_b):
    bsz, t, d = x.shape

    def ffn(h3, w_in, w_out, g, b):
        return _ffn_ln(h3.reshape(bsz * t, d), w_in, w_out, g, b).reshape(bsz, t, d)

    h = ffn(x, l0_ffn1_w_in, l0_ffn1_w_out, l0_ln1_g, l0_ln1_b)
    h = _gdn_layer(h, mem, l0_w_in, l0_conv_w, l0_a_log, l0_dt_bias, l0_onorm_g, l0_w_mem_kv, l0_w_o,
                   l0_ln2_g, l0_ln2_b)
    h = ffn(h, l0_ffn2_w_in, l0_ffn2_w_out, l0_ln3_g, l0_ln3_b)
    h = ffn(h, l1_ffn1_w_in, l1_ffn1_w_out, l1_ln1_g, l1_ln1_b)
    h = _dsa_layer(h, mem, l1_w_in, l1_kv_norm_g, l1_w_uk, l1_w_uv, l1_kidx_ln_g, l1_kidx_ln_b,
                   l1_w_mem_kv, l1_w_o, l1_ln2_g, l1_ln2_b)
    h = ffn(h, l1_ffn2_w_in, l1_ffn2_w_out, l1_ln3_g, l1_ln3_b)
    return h
```

```python
import functools
import math

import jax
import jax.numpy as jnp
from jax import lax
from jax.experimental import pallas as pl
from jax.experimental.pallas import tpu as pltpu

F32 = jnp.float32
BF16 = jnp.bfloat16
I32 = jnp.int32

DEPTH = 2
LN_EPS = 1e-5
ALPHA = (2 * DEPTH) ** 0.25
GDN_HEADS = 6
GDN_DK = 128
GDN_W = GDN_HEADS * GDN_DK
GDN_CONV = 4
DSA_HEADS = 12
DSA_DH = 64
DSA_W = DSA_HEADS * DSA_DH
DSA_KV_RANK = 256
IDX_HEADS = 8
IDX_DIM = 64
TOPK_MAX = 256
MEM_HEADS = 4
MEM_DH = 64
MEM_W = MEM_HEADS * MEM_DH

LANES = 128
SUBLANES = 8
VMEM_LIMIT_BYTES = 56 * 1024 * 1024

ROW_TILE = 512
FFN_ROW_TILE = 1024
FFN_SUB_TILES = 2
FFN_CHUNK = 256
GDN_CHUNK = 128
GDN_TILE = 256
INV_BLOCK = 16
DSA_TQ = 2 * LANES
DSA_TK = 512
NO_INDEX_BOUND = 2 ** 30
COUNT_ROWS = 8 * SUBLANES
SUM_ROWS = 2 * SUBLANES
RADIX_FIXED = 16
RADIX_CHECK = 4
LOG2E = math.log2(math.e)

def _cparams(sem):
    return pltpu.CompilerParams(dimension_semantics=sem, vmem_limit_bytes=VMEM_LIMIT_BYTES)


def _resident(shape):
    nd = len(shape)
    return pl.BlockSpec(shape, lambda *_: (0,) * nd, pipeline_mode=pl.Buffered(1))


def _mm(a, b):
    return jnp.dot(a.astype(BF16), b.astype(BF16), preferred_element_type=F32)


def _mm_nt(a, b):
    return lax.dot_general(a.astype(BF16), b.astype(BF16), (((1,), (1,)), ((), ())),
                           preferred_element_type=F32)


def _mm_tn(a, b):
    return lax.dot_general(a.astype(BF16), b.astype(BF16), (((0,), (0,)), ((), ())),
                           preferred_element_type=F32)


def _layer_norm(y, g, b):
    mu = jnp.mean(y, axis=-1, keepdims=True)
    d = y - mu
    var = jnp.mean(d * d, axis=-1, keepdims=True)
    return d * lax.rsqrt(var + LN_EPS) * g + b


def _silu(x):
    return x * jax.nn.sigmoid(x)


def _ffn_kernel(x_ref, wg_ref, wu_ref, wo_ref, g_ref, b_ref, o_ref, acc_ref, *, n_chunks, fc, n_sub):
    sub = x_ref.shape[0] // n_sub

    def finish(s):
        rows = slice(s * sub, (s + 1) * sub)
        o_ref[rows, :] = _layer_norm(ALPHA * x_ref[rows, :] + 0.5 * acc_ref[rows, :], g_ref[...], b_ref[...])

    for s in range(n_sub):
        rows = slice(s * sub, (s + 1) * sub)
        xb = x_ref[rows, :].astype(BF16)
        for c in range(n_chunks):
            cs = slice(c * fc, (c + 1) * fc)
            gate = jnp.dot(xb, wg_ref[:, cs], preferred_element_type=F32)
            up = jnp.dot(xb, wu_ref[:, cs], preferred_element_type=F32)
            part = jnp.dot((_silu(gate) * up).astype(BF16), wo_ref[cs, :], preferred_element_type=F32)
            if c == 0:
                acc_ref[rows, :] = part
            else:
                acc_ref[rows, :] += part
            if c == 0 and s > 0:
                finish(s - 1)
    finish(n_sub - 1)


def _prep_ffn(w_in, w_out):
    dff = w_out.shape[0]
    pad = -dff % FFN_CHUNK
    wg = jnp.pad(w_in[:, :dff].astype(BF16), ((0, 0), (0, pad)))
    wu = jnp.pad(w_in[:, dff:].astype(BF16), ((0, 0), (0, pad)))
    wo = jnp.pad(w_out.astype(BF16), ((0, pad), (0, 0)))
    return wg, wu, wo


def _ffn_ln(x2, w_in, w_out, g, b):
    m, d = x2.shape
    wg, wu, wo = _prep_ffn(w_in, w_out)
    bm = min(FFN_ROW_TILE, m)
    row = pl.BlockSpec((bm, d), lambda i: (i, 0))
    return pl.pallas_call(
        functools.partial(_ffn_kernel, n_chunks=wo.shape[0] // FFN_CHUNK, fc=FFN_CHUNK, n_sub=FFN_SUB_TILES),
        grid=(m // bm,),
        in_specs=[row, _resident(wg.shape), _resident(wu.shape), _resident(wo.shape),
                  _resident((1, d)), _resident((1, d))],
        out_specs=row,
        out_shape=jax.ShapeDtypeStruct((m, d), F32),
        scratch_shapes=[pltpu.VMEM((bm, d), F32)],
        compiler_params=_cparams(("parallel",)),
        name="ffn_ln",
    )(x2, wg, wu, wo, g.reshape(1, d), b.reshape(1, d))


def _proj_kernel(x_ref, w_ref, *o_refs, widths):
    xb = x_ref[...].astype(BF16)
    off = 0
    for o_ref, n in zip(o_refs, widths):
        o_ref[...] = jnp.dot(xb, w_ref[:, off:off + n], preferred_element_type=F32).astype(o_ref.dtype)
        off += n


def _proj(x2, w, widths, dtypes):
    m, d = x2.shape
    bm = min(ROW_TILE, m)
    return pl.pallas_call(
        functools.partial(_proj_kernel, widths=widths),
        grid=(m // bm,),
        in_specs=[pl.BlockSpec((bm, d), lambda i: (i, 0)), _resident(w.shape)],
        out_specs=[pl.BlockSpec((bm, n), lambda i: (i, 0)) for n in widths],
        out_shape=[jax.ShapeDtypeStruct((m, n), dt) for n, dt in zip(widths, dtypes)],
        compiler_params=_cparams(("parallel",)),
        name="proj",
    )(x2, w)


def _pad_cols(w, n):
    return jnp.pad(w, ((0, 0), (0, n - w.shape[1])))


def _unit_lower_inverse(mats, xor_idx, eye):
    c = mats[0].shape[0]
    shift = int(math.log2(INV_BLOCK))
    x = [jnp.where((xor_idx >> shift) == 0, a, 0.0) for a in mats]
    p = [eye - xi for xi in x]
    for _ in range(shift - 1):
        x = [_mm(xi, xi) for xi in x]
        p = [pi + _mm(pi, xi) for pi, xi in zip(p, x)]
    t = p
    while (1 << shift) < c:
        off = [jnp.where((xor_idx >> shift) == 1, a, 0.0) for a in mats]
        t_off = [_mm(ti, oi) for ti, oi in zip(t, off)]
        t = [ti - _mm(toi, ti) for ti, toi in zip(t, t_off)]
        shift += 1
    return t


def _gdn_kernel(alog_ref, dtb_ref, q_ref, k_ref, v_ref, gate_ref, small_ref,
                cwq_ref, cwk_ref, cwv_ref, og_ref, o_ref,
                s_ref, halo_ref, stage_ref, smt_ref, *, chunk, n_chunks):
    tc = chunk * n_chunks

    @pl.when(pl.program_id(1) == 0)
    def _():
        s_ref[...] = jnp.zeros_like(s_ref)
        halo_ref[...] = jnp.zeros_like(halo_ref)

    def conv_silu(a, raw_ref, cw_ref):
        raw = raw_ref[0]
        stage_ref[a, 0:SUBLANES, :] = halo_ref[a]
        stage_ref[a, SUBLANES:SUBLANES + tc, :] = raw
        halo_ref[a] = raw[tc - SUBLANES:tc, :]
        cw = cw_ref[...]
        y = cw[GDN_CONV - 1:GDN_CONV, :] * raw
        for j in range(GDN_CONV - 1):
            lo = SUBLANES - (GDN_CONV - 1) + j
            y = y + cw[j:j + 1, :] * stage_ref[a, lo:lo + tc, :]
        return _silu(y)

    q_all = conv_silu(0, q_ref, cwq_ref)
    k_all = conv_silu(1, k_ref, cwk_ref)
    v_all = conv_silu(2, v_ref, cwv_ref)

    smt_ref[...] = small_ref[0].T
    lane_in_chunk = lax.broadcasted_iota(I32, (SUBLANES, tc), 1) & (chunk - 1)

    def to_cols(row):
        return jnp.broadcast_to(row[0:1, :], (LANES, tc)).T

    ri = lax.broadcasted_iota(I32, (chunk, chunk), 0)
    ci = lax.broadcasted_iota(I32, (chunk, chunk), 1)
    incl = ri >= ci
    strict = ri > ci
    xor_idx = ri ^ ci
    eye = jnp.where(ri == ci, 1.0, 0.0).astype(F32)

    def each(f, *lists):
        return [f(*args) for args in zip(*lists)]

    heads = list(range(GDN_HEADS))
    col_of = [slice(h * GDN_DK, (h + 1) * GDN_DK) for h in heads]
    q_h = [q_all[:, cs] for cs in col_of]
    k_h = [k_all[:, cs] for cs in col_of]
    v_h = [v_all[:, cs] for cs in col_of]
    q_h = each(lambda q: q * lax.rsqrt(jnp.sum(q * q, axis=-1, keepdims=True) + 1e-6) * (GDN_DK ** -0.5), q_h)
    k_h = each(lambda k: k * lax.rsqrt(jnp.sum(k * k, axis=-1, keepdims=True) + 1e-6), k_h)
    beta_row = [jax.nn.sigmoid(jnp.broadcast_to(smt_ref[h:h + 1, :], (SUBLANES, tc))) for h in heads]
    g_row = [-jnp.exp(jnp.full((SUBLANES, tc), alog_ref[h], F32))
             * jax.nn.softplus(jnp.broadcast_to(smt_ref[GDN_HEADS + h:GDN_HEADS + h + 1, :], (SUBLANES, tc))
                               + dtb_ref[h]) for h in heads]
    gc_row = g_row
    step = 1
    while step < chunk:
        gc_row = each(lambda g: g + jnp.where(lane_in_chunk >= step, pltpu.roll(g, step, axis=1), 0.0), gc_row)
        step *= 2
    beta_c = each(to_cols, beta_row)
    gc_c = each(to_cols, gc_row)

    probs = [(c, h) for c in range(n_chunks) for h in heads]
    rows_of = [slice(c * chunk, (c + 1) * chunk) for c, _ in probs]
    qc = [q_h[h][rows_of[i]] for i, (_, h) in enumerate(probs)]
    kc = [k_h[h][rows_of[i]] for i, (_, h) in enumerate(probs)]
    vc = [v_h[h][rows_of[i]] for i, (_, h) in enumerate(probs)]
    bc = [beta_c[h][rows_of[i]] for i, (_, h) in enumerate(probs)]
    gcc = [gc_c[h][rows_of[i]] for i, (_, h) in enumerate(probs)]
    gcr = [gc_row[h][0:1, rows_of[i]] for i, (_, h) in enumerate(probs)]
    decay = each(lambda a, b: jnp.where(incl, jnp.exp(jnp.where(incl, a - b, 0.0)), 0.0), gcc, gcr)
    kb = each(lambda a, b: a * b, kc, bc)
    a_mat = each(lambda a, b, d: jnp.where(strict, _mm_nt(a, b) * d, 0.0), kb, kc, decay)
    t_mat = _unit_lower_inverse(a_mat, xor_idx, eye)
    eg = each(jnp.exp, gcc)
    uw = each(lambda t, v, b, kbi, e: _mm(t, jnp.concatenate([v * b, kbi * e], axis=1)), t_mat, vc, bc, kb, eg)
    qk = each(lambda a, b, d: jnp.where(incl, _mm_nt(a, b) * d, 0.0), qc, kc, decay)
    g_last = each(lambda g: g[chunk - 1:chunk, :], gcc)
    k_dec = each(lambda k, gl, g: k * jnp.exp(gl - g), kc, g_last, gcc)
    lhs = each(lambda w, q, e: jnp.concatenate([w[:, GDN_DK:], q * e], axis=0), uw, qc, eg)

    state = [s_ref[h] for h in heads]
    for c in range(n_chunks):
        idx = [c * GDN_HEADS + h for h in heads]
        ws = [_mm(lhs[i], state[h]) for i, h in zip(idx, heads)]
        u = [uw[i][:, :GDN_DK] - w[:chunk] for i, w in zip(idx, ws)]
        o = [w[chunk:] + _mm(qk[i], ui) for i, w, ui in zip(idx, ws, u)]
        state = [state[h] * jnp.exp(g_last[i]) + _mm_tn(k_dec[i], ui) for i, h, ui in zip(idx, heads, u)]
        o = each(lambda x: x * lax.rsqrt(jnp.mean(x * x, axis=-1, keepdims=True) + LN_EPS) * og_ref[...], o)
        for h in heads:
            o_ref[0, rows_of[idx[h]], col_of[h]] = o[h] * _silu(gate_ref[0, rows_of[idx[h]], col_of[h]])
    for h in heads:
        s_ref[h] = state[h]


def _gdn_core(qkv, gate, small, conv_w, a_log, dt_bias, onorm_g):
    b, t, _ = qkv.shape
    tc = min(GDN_TILE, t)
    n_chunks = tc // GDN_CHUNK
    hh = GDN_HEADS

    def tok_block(j):
        return pl.BlockSpec((1, tc, GDN_W), lambda bi, ti: (bi, ti, j))

    def conv_block(j):
        return pl.BlockSpec((GDN_CONV, GDN_W), lambda bi, ti: (0, j))

    smem = pl.BlockSpec(memory_space=pltpu.SMEM)
    return pl.pallas_call(
        functools.partial(_gdn_kernel, chunk=GDN_CHUNK, n_chunks=n_chunks),
        grid=(b, t // tc),
        in_specs=[smem, smem,
                  tok_block(0), tok_block(1), tok_block(2),
                  tok_block(0),
                  pl.BlockSpec((1, tc, LANES), lambda bi, ti: (bi, ti, 0)),
                  conv_block(0), conv_block(1), conv_block(2),
                  pl.BlockSpec((1, GDN_DK), lambda bi, ti: (0, 0))],
        out_specs=tok_block(0),
        out_shape=jax.ShapeDtypeStruct((b, t, GDN_W), F32),
        scratch_shapes=[pltpu.VMEM((hh, GDN_DK, GDN_DK), F32),
                        pltpu.VMEM((3, SUBLANES, GDN_W), F32),
                        pltpu.VMEM((3, SUBLANES + tc, GDN_W), F32),
                        pltpu.VMEM((LANES, tc), F32)],
        compiler_params=_cparams(("parallel", "arbitrary")),
        name="gdn_core",
    )(a_log, dt_bias, qkv, qkv, qkv, gate, small, conv_w, conv_w, conv_w, onorm_g.reshape(1, GDN_DK))


def _mix_out_kernel(x_ref, o_ref, qm_ref, kv_ref, wo1_ref, wo2_ref, g_ref, b_ref, out_ref, *, o_transposed):
    qm = qm_ref[0]
    kv = kv_ref[0]
    k_mem, v_mem = kv[:, :MEM_W], kv[:, MEM_W:]
    head_of_lane = lax.broadcasted_iota(I32, (1, MEM_W), 1) >> (MEM_DH.bit_length() - 1)
    y_mix = _mm_tn(o_ref[...], wo1_ref[...]) if o_transposed else _mm(o_ref[0], wo1_ref[...])
    mine = [head_of_lane == hd for hd in range(MEM_HEADS)]
    s = [_mm_nt(jnp.where(mk, qm, 0.0), k_mem) * (MEM_DH ** -0.5) for mk in mine]
    s = [si - jnp.max(si, axis=-1, keepdims=True) for si in s]
    p = [jnp.exp(si) for si in s]
    p = [pi / jnp.sum(pi, axis=-1, keepdims=True) for pi in p]
    pv = [_mm(pi, v_mem) for pi in p]
    o_mem = jnp.zeros(qm.shape, F32)
    for mk, pvi in zip(mine, pv):
        o_mem = o_mem + jnp.where(mk, pvi, 0.0)
    y = y_mix + _mm(o_mem, wo2_ref[...])
    out_ref[0] = _layer_norm(ALPHA * x_ref[0] + y, g_ref[...], b_ref[...])


def _mix_out(x3, o, qm3, kv3, w_o, g, b, *, o_transposed):
    bsz, t, d = x3.shape
    w1 = o.shape[0] if o_transposed else o.shape[-1]
    bm = min(ROW_TILE, t)
    nt = t // bm
    wo1 = w_o[:w1].astype(BF16)
    wo2 = w_o[w1:].astype(BF16)

    def rows(n):
        return pl.BlockSpec((1, bm, n), lambda bi, ti: (bi, ti, 0))

    o_spec = pl.BlockSpec((w1, bm), lambda bi, ti: (0, bi * nt + ti)) if o_transposed else rows(w1)
    return pl.pallas_call(
        functools.partial(_mix_out_kernel, o_transposed=o_transposed),
        grid=(bsz, nt),
        in_specs=[rows(d), o_spec, rows(MEM_W),
                  pl.BlockSpec((1,) + kv3.shape[1:], lambda bi, ti: (bi, 0, 0)),
                  _resident(wo1.shape), _resident(wo2.shape), _resident((1, d)), _resident((1, d))],
        out_specs=rows(d),
        out_shape=jax.ShapeDtypeStruct((bsz, t, d), F32),
        compiler_params=_cparams(("parallel", "parallel")),
        name="mix_out",
    )(x3, o, qm3, kv3, wo1, wo2, g.reshape(1, d), b.reshape(1, d))


def _dsa_proj_kernel(x_ref, w_ref, kvg_ref, lng_ref, lnb_ref,
                     qt_ref, ckv_ref, ckvt_ref, qidxt_ref, kidx_ref, wsmt_ref, qm_ref, *, tk):
    xb = x_ref[...].astype(BF16)
    bm = xb.shape[0]

    def cols(lo, n):
        return jnp.dot(xb, w_ref[:, lo:lo + n], preferred_element_type=F32)

    o1 = DSA_W
    o2 = o1 + DSA_KV_RANK
    o3 = o2 + IDX_HEADS * IDX_DIM
    qt_ref[...] = cols(0, o1).T.astype(qt_ref.dtype)
    c = cols(o1, DSA_KV_RANK)
    c = c * lax.rsqrt(jnp.mean(c * c, axis=-1, keepdims=True) + LN_EPS) * kvg_ref[...]
    ckv_ref[...] = c.astype(ckv_ref.dtype)
    c_t = c.T.astype(ckvt_ref.dtype)
    for j in range(bm // tk):
        ckvt_ref[j, 0:DSA_KV_RANK, :] = c_t[:, j * tk:(j + 1) * tk]
        ckvt_ref[j, DSA_KV_RANK:, :] = jnp.ones((SUM_ROWS, tk), ckvt_ref.dtype)
    qidxt_ref[...] = cols(o2, IDX_HEADS * IDX_DIM).T.astype(qidxt_ref.dtype)
    lane = lax.broadcasted_iota(I32, (1, LANES), 1)
    for half in range(2):
        kx = cols(o3 + half * LANES, LANES)
        live = (lane >= half * IDX_DIM) & (lane < (half + 1) * IDX_DIM)
        mu = jnp.sum(kx, axis=-1, keepdims=True) * (1.0 / IDX_DIM)
        d = jnp.where(live, kx - mu, 0.0)
        var = jnp.sum(d * d, axis=-1, keepdims=True) * (1.0 / IDX_DIM)
        kn = d * lax.rsqrt(var + LN_EPS) * lng_ref[:, half * LANES:(half + 1) * LANES] \
            + lnb_ref[:, half * LANES:(half + 1) * LANES]
        kidx_ref[:, half * LANES:(half + 1) * LANES] = jnp.where(live, kn, 0.0).astype(kidx_ref.dtype)
    w_t = (cols(o3 + 2 * LANES, LANES) * (IDX_HEADS ** -0.5 * IDX_DIM ** -0.5)).T
    wsmt_ref[...] = w_t[:IDX_HEADS, :]
    qm_ref[...] = cols(o3 + 3 * LANES, MEM_W)


def _dsa_proj(x2, w_in, kv_norm_g, kidx_g, kidx_b, tk):
    m, d = x2.shape
    o1 = DSA_W
    o2 = o1 + DSA_KV_RANK
    o3 = o2 + IDX_HEADS * IDX_DIM
    o4 = o3 + IDX_DIM
    o5 = o4 + IDX_HEADS
    zeros64 = jnp.zeros((d, IDX_DIM), w_in.dtype)
    w = jnp.concatenate([
        w_in[:, :o3],
        w_in[:, o3:o4], zeros64, zeros64, w_in[:, o3:o4],
        _pad_cols(w_in[:, o4:o5], LANES),
        w_in[:, o5:],
    ], axis=1).astype(BF16)
    z64 = jnp.zeros((IDX_DIM,), F32)
    lng = jnp.concatenate([kidx_g, z64, z64, kidx_g]).reshape(1, 2 * LANES)
    lnb = jnp.concatenate([kidx_b, z64, z64, kidx_b]).reshape(1, 2 * LANES)
    bm = min(ROW_TILE, m)
    nqi = IDX_HEADS * IDX_DIM
    rr = DSA_KV_RANK

    def tok_major(n):
        return pl.BlockSpec((bm, n), lambda i: (i, 0))

    def feat_major(n):
        return pl.BlockSpec((n, bm), lambda i: (0, i))

    return pl.pallas_call(
        functools.partial(_dsa_proj_kernel, tk=tk),
        grid=(m // bm,),
        in_specs=[tok_major(d), _resident(w.shape),
                  _resident((1, rr)), _resident((1, 2 * LANES)), _resident((1, 2 * LANES))],
        out_specs=[feat_major(DSA_W), tok_major(rr),
                   pl.BlockSpec((bm // tk, rr + SUM_ROWS, tk), lambda i: (i, 0, 0)),
                   feat_major(nqi), tok_major(2 * LANES), feat_major(IDX_HEADS), tok_major(MEM_W)],
        out_shape=[jax.ShapeDtypeStruct((DSA_W, m), BF16),
                   jax.ShapeDtypeStruct((m, rr), BF16),
                   jax.ShapeDtypeStruct((m // tk, rr + SUM_ROWS, tk), BF16),
                   jax.ShapeDtypeStruct((nqi, m), BF16),
                   jax.ShapeDtypeStruct((m, 2 * LANES), BF16),
                   jax.ShapeDtypeStruct((IDX_HEADS, m), F32),
                   jax.ShapeDtypeStruct((m, MEM_W), F32)],
        compiler_params=_cparams(("parallel",)),
        name="dsa_proj",
    )(x2, w, kv_norm_g.reshape(1, rr), lng, lnb)


def _ordered_key_to_f32(u):
    key = u ^ jnp.int32(-2 ** 31)
    bits = key ^ ((key >> 31) & jnp.int32(0x7FFFFFFF))
    return lax.bitcast_convert_type(bits, F32)


def _dsa_attn_kernel(qt_ref, qidxt_ref, wsmt_ref, kidx_ref, ckv_ref, ckvt_ref, wukt_ref, wuvt_ref, o_ref,
                     sc_ref, b0_ref, acc_ref, pos_ref, *, tq, tk, topk, pos_bits):
    hh, rr, dh = DSA_HEADS, DSA_KV_RANK, DSA_DH
    t0 = pl.program_id(1) * tq
    nk = (t0 + tq + tk - 1) // tk
    key_i = lax.broadcasted_iota(I32, (tk, tq), 0)
    t_pos = t0 + lax.broadcasted_iota(I32, (tk, tq), 1)
    neg_inf = jnp.float32(-jnp.inf)

    pairs = IDX_HEADS // 2
    lhs_t = jnp.concatenate([qidxt_ref[p * LANES:(p + 1) * LANES, :] for p in range(pairs)], axis=1)
    w_t = wsmt_ref[...]
    w_rows = [jnp.concatenate([w_t[2 * p + half:2 * p + half + 1, :] for p in range(pairs)], axis=1)
              for half in range(2)]

    def idx_body(kb, carry):
        k0 = pl.multiple_of(kb * tk, tk)
        kblk = kidx_ref[0, pl.ds(k0, tk), :]
        acc = jnp.zeros((tk, tq), F32)
        for half in range(2):
            z = jnp.dot(kblk[:, half * LANES:(half + 1) * LANES], lhs_t, preferred_element_type=F32)
            z = jnp.maximum(z, 0.0) * w_rows[half]
            for p in range(pairs):
                acc = acc + z[:, p * tq:(p + 1) * tq]
        sc_ref[kb] = jnp.where(k0 + key_i <= t_pos, acc, neg_inf)
        return carry

    lax.fori_loop(0, nk, idx_body, 0)

    n_valid = (t0 + lax.broadcasted_iota(I32, (1, tq), 1) + 1).astype(F32)
    k_row = jnp.minimum(jnp.float32(topk), n_valid)

    def count(pred):
        def body(kb, acc):
            hit = jnp.where(pred(sc_ref[kb], kb), jnp.float32(1.0), jnp.float32(0.0))
            return acc + jnp.sum(hit.reshape(tk // COUNT_ROWS, COUNT_ROWS, tq), axis=0)
        acc = lax.fori_loop(0, nk, body, jnp.zeros((COUNT_ROWS, tq), F32))
        return jnp.sum(acc, axis=0, keepdims=True)

    def radix_step(it, c):
        tau_u, done = c
        cand_u = tau_u | jnp.left_shift(jnp.int32(1), 31 - it)
        cand = _ordered_key_to_f32(cand_u)
        cnt = count(lambda blk, kb: blk >= cand)
        tau_u = jnp.where((done == 0.0) & (cnt >= k_row), cand_u, tau_u)
        return tau_u, jnp.where(cnt == k_row, 1.0, done)

    def radix_chunk(c):
        it, tau_u, done, _ = c
        tau_u, done = lax.fori_loop(0, RADIX_CHECK, lambda j, s: radix_step(it + j, s), (tau_u, done))
        return it + RADIX_CHECK, tau_u, done, jnp.sum(1.0 - done)

    tau_u, done = lax.fori_loop(0, RADIX_FIXED, radix_step,
                                (jnp.zeros((1, tq), I32), jnp.zeros((1, tq), F32)))
    _, tau_u, _, n_open = lax.while_loop(
        lambda c: (c[0] < 32) & (c[3] > 0.0), radix_chunk,
        (jnp.int32(RADIX_FIXED), tau_u, done, jnp.sum(1.0 - done)))
    tau = _ordered_key_to_f32(tau_u)

    pos_ref[...] = jnp.full((1, tq), NO_INDEX_BOUND, I32)

    @pl.when(n_open > 0.0)
    def _():
        need = k_row - count(lambda blk, kb: blk > tau)

        def pos_body(it, pos):
            cand = pos | jnp.left_shift(jnp.int32(1), pos_bits - 1 - it)
            cnt = count(lambda blk, kb: (blk == tau) & (kb * tk + key_i < cand))
            return jnp.where(cnt < need, cand, pos)

        pos_ref[...] = lax.fori_loop(0, pos_bits, pos_body, jnp.zeros((1, tq), I32))

    pos = pos_ref[...]

    qt = qt_ref[...]
    q_lat_t = jnp.concatenate(
        [jnp.dot(wukt_ref[hd], qt[hd * dh:(hd + 1) * dh, :], preferred_element_type=F32)
         for hd in range(hh)], axis=1)
    q_lat_t = (q_lat_t * (dh ** -0.5 * LOG2E)).astype(BF16)
    slopes = [2.0 ** (-8.0 * (hd + 1) / hh) * LOG2E for hd in range(hh)]
    slope_row = jnp.concatenate([jnp.full((1, tq), sl, F32) for sl in slopes], axis=1)
    key_f = key_i.astype(F32)
    for hd in range(hh):
        b0_ref[:, hd * tq:(hd + 1) * tq] = key_f * slopes[hd]
    def masked_logits(kb):
        k0 = pl.multiple_of(kb * tk, tk)
        s = jnp.dot(ckv_ref[0, pl.ds(k0, tk), :], q_lat_t, preferred_element_type=F32)
        blk = sc_ref[kb]
        sel = (blk > tau) | ((blk == tau) & (k0 + key_i <= pos))
        bias = jnp.where(sel, 0.0, neg_inf)
        x = s + b0_ref[...] + jnp.concatenate([bias] * hh, axis=1)
        return x, slope_row * (k0 - t0).astype(F32)

    def att_body(kb, m_old):
        x, r = masked_logits(kb)
        m_new = jnp.maximum(m_old, jnp.max(x, axis=0, keepdims=True) + r)
        m_safe = jnp.where(m_new == neg_inf, 0.0, m_new)
        p = jnp.exp2(x - (m_safe - r))
        acc_ref[...] = jnp.exp2(m_old - m_safe) * acc_ref[...] + jnp.dot(
            ckvt_ref[0, kb], p.astype(BF16), preferred_element_type=F32)
        return m_new

    acc_ref[...] = jnp.zeros(acc_ref.shape, F32)
    lax.fori_loop(0, nk, att_body, jnp.full((1, hh * tq), neg_inf, F32))
    o_lat_t = (acc_ref[0:rr, :] / acc_ref[rr:rr + 1, :]).astype(BF16)
    for hd in range(hh):
        o_ref[hd * dh:(hd + 1) * dh, :] = jnp.dot(
            wuvt_ref[hd], o_lat_t[:, hd * tq:(hd + 1) * tq], preferred_element_type=F32).astype(o_ref.dtype)


def _dsa_attn(b, t, qt, qidxt, wsmt, kidx3, ckv3, ckvt4, w_uk, w_uv, tk):
    tq = min(DSA_TQ, t)
    nq = t // tq
    topk = min(TOPK_MAX, t // 4)
    hh, rr = DSA_HEADS, DSA_KV_RANK
    wukt = w_uk.transpose(0, 2, 1).astype(BF16)
    wuvt = w_uv.transpose(0, 2, 1).astype(BF16)

    def qcols(n):
        return pl.BlockSpec((n, tq), lambda bi, ti: (0, bi * nq + ti))

    return pl.pallas_call(
        functools.partial(_dsa_attn_kernel, tq=tq, tk=tk, topk=topk,
                          pos_bits=max(1, (t - 1).bit_length())),
        grid=(b, nq),
        in_specs=[qcols(DSA_W), qcols(IDX_HEADS * IDX_DIM), qcols(IDX_HEADS),
                  pl.BlockSpec((1, t, 2 * LANES), lambda bi, ti: (bi, 0, 0)),
                  pl.BlockSpec((1, t, rr), lambda bi, ti: (bi, 0, 0)),
                  pl.BlockSpec((1, t // tk, rr + SUM_ROWS, tk), lambda bi, ti: (bi, 0, 0, 0)),
                  _resident(wukt.shape), _resident(wuvt.shape)],
        out_specs=qcols(DSA_W),
        out_shape=jax.ShapeDtypeStruct((DSA_W, b * t), BF16),
        scratch_shapes=[pltpu.VMEM((t // tk, tk, tq), F32),
                        pltpu.VMEM((tk, hh * tq), F32),
                        pltpu.VMEM((rr + SUM_ROWS, hh * tq), F32),
                        pltpu.VMEM((1, tq), I32)],
        compiler_params=_cparams(("parallel", "arbitrary")),
        name="dsa_attn",
    )(qt, qidxt, wsmt, kidx3, ckv3, ckvt4, wukt, wuvt)


def _mem_kv(mem, w_mem_kv):
    b, mlen, d = mem.shape
    (kv,) = _proj(mem.reshape(b * mlen, d), w_mem_kv.astype(BF16), (2 * MEM_W,), (BF16,))
    return kv.reshape(b, mlen, 2 * MEM_W)


def _gdn_layer(x3, mem, w_in, conv_w, a_log, dt_bias, onorm_g, w_mem_kv, w_o, g, b):
    bsz, t, d = x3.shape
    n_qkv = 3 * GDN_W
    n_gate = 4 * GDN_W
    n_small = n_gate + 2 * GDN_HEADS
    w = jnp.concatenate([w_in[:, :n_gate], _pad_cols(w_in[:, n_gate:n_small], LANES),
                         w_in[:, n_small:]], axis=1).astype(BF16)
    qkv, gate, small, q_mem = _proj(x3.reshape(bsz * t, d), w, (n_qkv, GDN_W, LANES, MEM_W),
                                    (F32, F32, F32, F32))
    o = _gdn_core(qkv.reshape(bsz, t, n_qkv), gate.reshape(bsz, t, GDN_W), small.reshape(bsz, t, LANES),
                  conv_w, a_log, dt_bias, onorm_g)
    return _mix_out(x3, o, q_mem.reshape(bsz, t, MEM_W), _mem_kv(mem, w_mem_kv), w_o, g, b,
                    o_transposed=False)


def _dsa_layer(x3, mem, w_in, kv_norm_g, w_uk, w_uv, kidx_g, kidx_b, w_mem_kv, w_o, g, b):
    bsz, t, d = x3.shape
    tk = min(DSA_TK, t)
    qt, ckv, ckvt, qidxt, kidx, wsmt, q_mem = _dsa_proj(x3.reshape(bsz * t, d), w_in, kv_norm_g,
                                                       kidx_g, kidx_b, tk)
    o_t = _dsa_attn(bsz, t, qt, qidxt, wsmt,
                    kidx.reshape(bsz, t, 2 * LANES), ckv.reshape(bsz, t, DSA_KV_RANK),
                    ckvt.reshape(bsz, t // tk, DSA_KV_RANK + SUM_ROWS, tk), w_uk, w_uv, tk)
    return _mix_out(x3, o_t, q_mem.reshape(bsz, t, MEM_W), _mem_kv(mem, w_mem_kv), w_o, g, b,
                    o_transposed=True)


def kernel(x, mem, l0_ffn1_w_in, l0_ffn1_w_out, l0_ln1_g, l0_ln1_b, l0_w_in, l0_conv_w, l0_a_log, l0_dt_bias, l0_onorm_g, l0_w_mem_kv, l0_w_o, l0_ln2_g, l0_ln2_b, l0_ffn2_w_in, l0_ffn2_w_out, l0_ln3_g, l0_ln3_b, l1_ffn1_w_in, l1_ffn1_w_out, l1_ln1_g, l1_ln1_b, l1_w_in, l1_kv_norm_g, l1_w_uk, l1_w_uv, l1_kidx_ln_g, l1_kidx_ln_b, l1_w_mem_kv, l1_w_o, l1_ln2_g, l1_ln2_b, l1_ffn2_w_in, l1_ffn2_w_out, l1_ln3_g, l1_ln3_b):
    bsz, t, d = x.shape

    def ffn(h3, w_in, w_out, g, b):
        return _ffn_ln(h3.reshape(bsz * t, d), w_in, w_out, g, b).reshape(bsz, t, d)

    h = ffn(x, l0_ffn1_w_in, l0_ffn1_w_out, l0_ln1_g, l0_ln1_b)
    h = _gdn_layer(h, mem, l0_w_in, l0_conv_w, l0_a_log, l0_dt_bias, l0_onorm_g, l0_w_mem_kv, l0_w_o,
                   l0_ln2_g, l0_ln2_b)
    h = ffn(h, l0_ffn2_w_in, l0_ffn2_w_out, l0_ln3_g, l0_ln3_b)
    h = ffn(h, l1_ffn1_w_in, l1_ffn1_w_out, l1_ln1_g, l1_ln1_b)
    h = _dsa_layer(h, mem, l1_w_in, l1_kv_norm_g, l1_w_uk, l1_w_uv, l1_kidx_ln_g, l1_kidx_ln_b,
                   l1_w_mem_kv, l1_w_o, l1_ln2_g, l1_ln2_b)
    h = ffn(h, l1_ffn2_w_in, l1_ffn2_w_out, l1_ln3_g, l1_ln3_b)
    return h
```

```python
import functools
import math

import jax
import jax.numpy as jnp
from jax import lax
from jax.experimental import pallas as pl
from jax.experimental.pallas import tpu as pltpu

F32 = jnp.float32
BF16 = jnp.bfloat16
I32 = jnp.int32

DEPTH = 2
LN_EPS = 1e-5
ALPHA = (2 * DEPTH) ** 0.25
GDN_HEADS = 6
GDN_DK = 128
GDN_W = GDN_HEADS * GDN_DK
GDN_CONV = 4
DSA_HEADS = 12
DSA_DH = 64
DSA_W = DSA_HEADS * DSA_DH
DSA_KV_RANK = 256
IDX_HEADS = 8
IDX_DIM = 64
TOPK_MAX = 256
MEM_HEADS = 4
MEM_DH = 64
MEM_W = MEM_HEADS * MEM_DH

LANES = 128
SUBLANES = 8
VMEM_LIMIT_BYTES = 56 * 1024 * 1024

ROW_TILE = 512
FFN_ROW_TILE = 1024
FFN_SUB_TILES = 2
FFN_CHUNK = 256
GDN_CHUNK = 128
GDN_TILE = 256
INV_BLOCK = 16
DSA_TQ = 2 * LANES
DSA_TK = 512
NO_INDEX_BOUND = 2 ** 30
COUNT_ROWS = 4 * SUBLANES
SUM_ROWS = 2 * SUBLANES
RADIX_FIXED = 16
RADIX_CHECK = 4
LOG2E = math.log2(math.e)

def _cparams(sem):
    return pltpu.CompilerParams(dimension_semantics=sem, vmem_limit_bytes=VMEM_LIMIT_BYTES)


def _resident(shape):
    nd = len(shape)
    return pl.BlockSpec(shape, lambda *_: (0,) * nd, pipeline_mode=pl.Buffered(1))


def _mm(a, b):
    return jnp.dot(a.astype(BF16), b.astype(BF16), preferred_element_type=F32)


def _mm_nt(a, b):
    return lax.dot_general(a.astype(BF16), b.astype(BF16), (((1,), (1,)), ((), ())),
                           preferred_element_type=F32)


def _mm_tn(a, b):
    return lax.dot_general(a.astype(BF16), b.astype(BF16), (((0,), (0,)), ((), ())),
                           preferred_element_type=F32)


def _layer_norm(y, g, b):
    mu = jnp.mean(y, axis=-1, keepdims=True)
    d = y - mu
    var = jnp.mean(d * d, axis=-1, keepdims=True)
    return d * lax.rsqrt(var + LN_EPS) * g + b


def _silu(x):
    return x * jax.nn.sigmoid(x)


def _ffn_kernel(x_ref, wg_ref, wu_ref, wo_ref, g_ref, b_ref, o_ref, acc_ref, *, n_chunks, fc, n_sub):
    sub = x_ref.shape[0] // n_sub

    def finish(s):
        rows = slice(s * sub, (s + 1) * sub)
        o_ref[rows, :] = _layer_norm(ALPHA * x_ref[rows, :] + 0.5 * acc_ref[rows, :], g_ref[...], b_ref[...])

    for s in range(n_sub):
        rows = slice(s * sub, (s + 1) * sub)
        xb = x_ref[rows, :].astype(BF16)
        for c in range(n_chunks):
            cs = slice(c * fc, (c + 1) * fc)
            gate = jnp.dot(xb, wg_ref[:, cs], preferred_element_type=F32)
            up = jnp.dot(xb, wu_ref[:, cs], preferred_element_type=F32)
            part = jnp.dot((_silu(gate) * up).astype(BF16), wo_ref[cs, :], preferred_element_type=F32)
            if c == 0:
                acc_ref[rows, :] = part
            else:
                acc_ref[rows, :] += part
            if c == 0 and s > 0:
                finish(s - 1)
    finish(n_sub - 1)


def _prep_ffn(w_in, w_out):
    dff = w_out.shape[0]
    pad = -dff % FFN_CHUNK
    wg = jnp.pad(w_in[:, :dff].astype(BF16), ((0, 0), (0, pad)))
    wu = jnp.pad(w_in[:, dff:].astype(BF16), ((0, 0), (0, pad)))
    wo = jnp.pad(w_out.astype(BF16), ((0, pad), (0, 0)))
    return wg, wu, wo


def _ffn_ln(x2, w_in, w_out, g, b):
    m, d = x2.shape
    wg, wu, wo = _prep_ffn(w_in, w_out)
    bm = min(FFN_ROW_TILE, m)
    row = pl.BlockSpec((bm, d), lambda i: (i, 0))
    return pl.pallas_call(
        functools.partial(_ffn_kernel, n_chunks=wo.shape[0] // FFN_CHUNK, fc=FFN_CHUNK, n_sub=FFN_SUB_TILES),
        grid=(m // bm,),
        in_specs=[row, _resident(wg.shape), _resident(wu.shape), _resident(wo.shape),
                  _resident((1, d)), _resident((1, d))],
        out_specs=row,
        out_shape=jax.ShapeDtypeStruct((m, d), F32),
        scratch_shapes=[pltpu.VMEM((bm, d), F32)],
        compiler_params=_cparams(("parallel",)),
        name="ffn_ln",
    )(x2, wg, wu, wo, g.reshape(1, d), b.reshape(1, d))


def _proj_kernel(x_ref, w_ref, *o_refs, widths):
    xb = x_ref[...].astype(BF16)
    off = 0
    for o_ref, n in zip(o_refs, widths):
        o_ref[...] = jnp.dot(xb, w_ref[:, off:off + n], preferred_element_type=F32).astype(o_ref.dtype)
        off += n


def _proj(x2, w, widths, dtypes):
    m, d = x2.shape
    bm = min(ROW_TILE, m)
    return pl.pallas_call(
        functools.partial(_proj_kernel, widths=widths),
        grid=(m // bm,),
        in_specs=[pl.BlockSpec((bm, d), lambda i: (i, 0)), _resident(w.shape)],
        out_specs=[pl.BlockSpec((bm, n), lambda i: (i, 0)) for n in widths],
        out_shape=[jax.ShapeDtypeStruct((m, n), dt) for n, dt in zip(widths, dtypes)],
        compiler_params=_cparams(("parallel",)),
        name="proj",
    )(x2, w)


def _pad_cols(w, n):
    return jnp.pad(w, ((0, 0), (0, n - w.shape[1])))


def _unit_lower_inverse(mats, xor_idx, eye):
    c = mats[0].shape[0]
    shift = int(math.log2(INV_BLOCK))
    x = [jnp.where((xor_idx >> shift) == 0, a, 0.0) for a in mats]
    p = [eye - xi for xi in x]
    for _ in range(shift - 1):
        x = [_mm(xi, xi) for xi in x]
        p = [pi + _mm(pi, xi) for pi, xi in zip(p, x)]
    t = p
    while (1 << shift) < c:
        off = [jnp.where((xor_idx >> shift) == 1, a, 0.0) for a in mats]
        t_off = [_mm(ti, oi) for ti, oi in zip(t, off)]
        t = [ti - _mm(toi, ti) for ti, toi in zip(t, t_off)]
        shift += 1
    return t


def _gdn_kernel(alog_ref, dtb_ref, q_ref, k_ref, v_ref, gate_ref, small_ref,
                cwq_ref, cwk_ref, cwv_ref, og_ref, o_ref,
                s_ref, halo_ref, stage_ref, smt_ref, *, chunk, n_chunks):
    tc = chunk * n_chunks

    @pl.when(pl.program_id(1) == 0)
    def _():
        s_ref[...] = jnp.zeros_like(s_ref)
        halo_ref[...] = jnp.zeros_like(halo_ref)

    def conv_silu(a, raw_ref, cw_ref):
        raw = raw_ref[0]
        stage_ref[a, 0:SUBLANES, :] = halo_ref[a]
        stage_ref[a, SUBLANES:SUBLANES + tc, :] = raw
        halo_ref[a] = raw[tc - SUBLANES:tc, :]
        cw = cw_ref[...]
        y = cw[GDN_CONV - 1:GDN_CONV, :] * raw
        for j in range(GDN_CONV - 1):
            lo = SUBLANES - (GDN_CONV - 1) + j
            y = y + cw[j:j + 1, :] * stage_ref[a, lo:lo + tc, :]
        return _silu(y)

    q_all = conv_silu(0, q_ref, cwq_ref)
    k_all = conv_silu(1, k_ref, cwk_ref)
    v_all = conv_silu(2, v_ref, cwv_ref)

    smt_ref[...] = small_ref[0].T
    lane_in_chunk = lax.broadcasted_iota(I32, (SUBLANES, tc), 1) & (chunk - 1)

    def to_cols(row):
        return jnp.broadcast_to(row[0:1, :], (LANES, tc)).T

    ri = lax.broadcasted_iota(I32, (chunk, chunk), 0)
    ci = lax.broadcasted_iota(I32, (chunk, chunk), 1)
    incl = ri >= ci
    strict = ri > ci
    xor_idx = ri ^ ci
    eye = jnp.where(ri == ci, 1.0, 0.0).astype(F32)

    def each(f, *lists):
        return [f(*args) for args in zip(*lists)]

    heads = list(range(GDN_HEADS))
    col_of = [slice(h * GDN_DK, (h + 1) * GDN_DK) for h in heads]
    q_h = [q_all[:, cs] for cs in col_of]
    k_h = [k_all[:, cs] for cs in col_of]
    v_h = [v_all[:, cs] for cs in col_of]
    q_h = each(lambda q: q * lax.rsqrt(jnp.sum(q * q, axis=-1, keepdims=True) + 1e-6) * (GDN_DK ** -0.5), q_h)
    k_h = each(lambda k: k * lax.rsqrt(jnp.sum(k * k, axis=-1, keepdims=True) + 1e-6), k_h)
    beta_row = [jax.nn.sigmoid(jnp.broadcast_to(smt_ref[h:h + 1, :], (SUBLANES, tc))) for h in heads]
    g_row = [-jnp.exp(jnp.full((SUBLANES, tc), alog_ref[h], F32))
             * jax.nn.softplus(jnp.broadcast_to(smt_ref[GDN_HEADS + h:GDN_HEADS + h + 1, :], (SUBLANES, tc))
                               + dtb_ref[h]) for h in heads]
    gc_row = g_row
    step = 1
    while step < chunk:
        gc_row = each(lambda g: g + jnp.where(lane_in_chunk >= step, pltpu.roll(g, step, axis=1), 0.0), gc_row)
        step *= 2
    beta_c = each(to_cols, beta_row)
    gc_c = each(to_cols, gc_row)

    probs = [(c, h) for c in range(n_chunks) for h in heads]
    rows_of = [slice(c * chunk, (c + 1) * chunk) for c, _ in probs]
    qc = [q_h[h][rows_of[i]] for i, (_, h) in enumerate(probs)]
    kc = [k_h[h][rows_of[i]] for i, (_, h) in enumerate(probs)]
    vc = [v_h[h][rows_of[i]] for i, (_, h) in enumerate(probs)]
    bc = [beta_c[h][rows_of[i]] for i, (_, h) in enumerate(probs)]
    gcc = [gc_c[h][rows_of[i]] for i, (_, h) in enumerate(probs)]
    gcr = [gc_row[h][0:1, rows_of[i]] for i, (_, h) in enumerate(probs)]
    decay = each(lambda a, b: jnp.where(incl, jnp.exp(jnp.where(incl, a - b, 0.0)), 0.0), gcc, gcr)
    kb = each(lambda a, b: a * b, kc, bc)
    a_mat = each(lambda a, b, d: jnp.where(strict, _mm_nt(a, b) * d, 0.0), kb, kc, decay)
    t_mat = _unit_lower_inverse(a_mat, xor_idx, eye)
    eg = each(jnp.exp, gcc)
    uw = each(lambda t, v, b, kbi, e: _mm(t, jnp.concatenate([v * b, kbi * e], axis=1)), t_mat, vc, bc, kb, eg)
    qk = each(lambda a, b, d: jnp.where(incl, _mm_nt(a, b) * d, 0.0), qc, kc, decay)
    g_last = each(lambda g: g[chunk - 1:chunk, :], gcc)
    k_dec = each(lambda k, gl, g: k * jnp.exp(gl - g), kc, g_last, gcc)
    lhs = each(lambda w, q, e: jnp.concatenate([w[:, GDN_DK:], q * e], axis=0), uw, qc, eg)

    state = [s_ref[h] for h in heads]
    for c in range(n_chunks):
        idx = [c * GDN_HEADS + h for h in heads]
        ws = [_mm(lhs[i], state[h]) for i, h in zip(idx, heads)]
        u = [uw[i][:, :GDN_DK] - w[:chunk] for i, w in zip(idx, ws)]
        o = [w[chunk:] + _mm(qk[i], ui) for i, w, ui in zip(idx, ws, u)]
        state = [state[h] * jnp.exp(g_last[i]) + _mm_tn(k_dec[i], ui) for i, h, ui in zip(idx, heads, u)]
        o = each(lambda x: x * lax.rsqrt(jnp.mean(x * x, axis=-1, keepdims=True) + LN_EPS) * og_ref[...], o)
        for h in heads:
            o_ref[0, rows_of[idx[h]], col_of[h]] = o[h] * _silu(gate_ref[0, rows_of[idx[h]], col_of[h]])
    for h in heads:
        s_ref[h] = state[h]


def _gdn_core(qkv, gate, small, conv_w, a_log, dt_bias, onorm_g):
    b, t, _ = qkv.shape
    tc = min(GDN_TILE, t)
    n_chunks = tc // GDN_CHUNK
    hh = GDN_HEADS

    def tok_block(j):
        return pl.BlockSpec((1, tc, GDN_W), lambda bi, ti: (bi, ti, j))

    def conv_block(j):
        return pl.BlockSpec((GDN_CONV, GDN_W), lambda bi, ti: (0, j))

    smem = pl.BlockSpec(memory_space=pltpu.SMEM)
    return pl.pallas_call(
        functools.partial(_gdn_kernel, chunk=GDN_CHUNK, n_chunks=n_chunks),
        grid=(b, t // tc),
        in_specs=[smem, smem,
                  tok_block(0), tok_block(1), tok_block(2),
                  tok_block(0),
                  pl.BlockSpec((1, tc, LANES), lambda bi, ti: (bi, ti, 0)),
                  conv_block(0), conv_block(1), conv_block(2),
                  pl.BlockSpec((1, GDN_DK), lambda bi, ti: (0, 0))],
        out_specs=tok_block(0),
        out_shape=jax.ShapeDtypeStruct((b, t, GDN_W), F32),
        scratch_shapes=[pltpu.VMEM((hh, GDN_DK, GDN_DK), F32),
                        pltpu.VMEM((3, SUBLANES, GDN_W), F32),
                        pltpu.VMEM((3, SUBLANES + tc, GDN_W), F32),
                        pltpu.VMEM((LANES, tc), F32)],
        compiler_params=_cparams(("parallel", "arbitrary")),
        name="gdn_core",
    )(a_log, dt_bias, qkv, qkv, qkv, gate, small, conv_w, conv_w, conv_w, onorm_g.reshape(1, GDN_DK))


def _mix_out_kernel(x_ref, o_ref, qm_ref, kv_ref, wo1_ref, wo2_ref, g_ref, b_ref, out_ref, *, o_transposed):
    qm = qm_ref[0]
    kv = kv_ref[0]
    k_mem, v_mem = kv[:, :MEM_W], kv[:, MEM_W:]
    head_of_lane = lax.broadcasted_iota(I32, (1, MEM_W), 1) >> (MEM_DH.bit_length() - 1)
    y_mix = _mm_tn(o_ref[...], wo1_ref[...]) if o_transposed else _mm(o_ref[0], wo1_ref[...])
    mine = [head_of_lane == hd for hd in range(MEM_HEADS)]
    s = [_mm_nt(jnp.where(mk, qm, 0.0), k_mem) * (MEM_DH ** -0.5) for mk in mine]
    s = [si - jnp.max(si, axis=-1, keepdims=True) for si in s]
    p = [jnp.exp(si) for si in s]
    p = [pi / jnp.sum(pi, axis=-1, keepdims=True) for pi in p]
    pv = [_mm(pi, v_mem) for pi in p]
    o_mem = jnp.zeros(qm.shape, F32)
    for mk, pvi in zip(mine, pv):
        o_mem = o_mem + jnp.where(mk, pvi, 0.0)
    y = y_mix + _mm(o_mem, wo2_ref[...])
    out_ref[0] = _layer_norm(ALPHA * x_ref[0] + y, g_ref[...], b_ref[...])


def _mix_out(x3, o, qm3, kv3, w_o, g, b, *, o_transposed):
    bsz, t, d = x3.shape
    w1 = o.shape[0] if o_transposed else o.shape[-1]
    bm = min(ROW_TILE, t)
    nt = t // bm
    wo1 = w_o[:w1].astype(BF16)
    wo2 = w_o[w1:].astype(BF16)

    def rows(n):
        return pl.BlockSpec((1, bm, n), lambda bi, ti: (bi, ti, 0))

    o_spec = pl.BlockSpec((w1, bm), lambda bi, ti: (0, bi * nt + ti)) if o_transposed else rows(w1)
    return pl.pallas_call(
        functools.partial(_mix_out_kernel, o_transposed=o_transposed),
        grid=(bsz, nt),
        in_specs=[rows(d), o_spec, rows(MEM_W),
                  pl.BlockSpec((1,) + kv3.shape[1:], lambda bi, ti: (bi, 0, 0)),
                  _resident(wo1.shape), _resident(wo2.shape), _resident((1, d)), _resident((1, d))],
        out_specs=rows(d),
        out_shape=jax.ShapeDtypeStruct((bsz, t, d), F32),
        compiler_params=_cparams(("parallel", "parallel")),
        name="mix_out",
    )(x3, o, qm3, kv3, wo1, wo2, g.reshape(1, d), b.reshape(1, d))


def _dsa_proj_kernel(x_ref, w_ref, kvg_ref, lng_ref, lnb_ref,
                     qt_ref, ckv_ref, ckvt_ref, qidxt_ref, kidx_ref, wsmt_ref, qm_ref, *, tk):
    xb = x_ref[...].astype(BF16)
    bm = xb.shape[0]

    def cols(lo, n):
        return jnp.dot(xb, w_ref[:, lo:lo + n], preferred_element_type=F32)

    o1 = DSA_W
    o2 = o1 + DSA_KV_RANK
    o3 = o2 + IDX_HEADS * IDX_DIM
    qt_ref[...] = cols(0, o1).T.astype(qt_ref.dtype)
    c = cols(o1, DSA_KV_RANK)
    c = c * lax.rsqrt(jnp.mean(c * c, axis=-1, keepdims=True) + LN_EPS) * kvg_ref[...]
    ckv_ref[...] = c.astype(ckv_ref.dtype)
    c_t = c.T.astype(ckvt_ref.dtype)
    for j in range(bm // tk):
        ckvt_ref[j, 0:DSA_KV_RANK, :] = c_t[:, j * tk:(j + 1) * tk]
        ckvt_ref[j, DSA_KV_RANK:, :] = jnp.ones((SUM_ROWS, tk), ckvt_ref.dtype)
    qidxt_ref[...] = cols(o2, IDX_HEADS * IDX_DIM).T.astype(qidxt_ref.dtype)
    lane = lax.broadcasted_iota(I32, (1, LANES), 1)
    for half in range(2):
        kx = cols(o3 + half * LANES, LANES)
        live = (lane >= half * IDX_DIM) & (lane < (half + 1) * IDX_DIM)
        mu = jnp.sum(kx, axis=-1, keepdims=True) * (1.0 / IDX_DIM)
        d = jnp.where(live, kx - mu, 0.0)
        var = jnp.sum(d * d, axis=-1, keepdims=True) * (1.0 / IDX_DIM)
        kn = d * lax.rsqrt(var + LN_EPS) * lng_ref[:, half * LANES:(half + 1) * LANES] \
            + lnb_ref[:, half * LANES:(half + 1) * LANES]
        kidx_ref[:, half * LANES:(half + 1) * LANES] = jnp.where(live, kn, 0.0).astype(kidx_ref.dtype)
    w_t = (cols(o3 + 2 * LANES, LANES) * (IDX_HEADS ** -0.5 * IDX_DIM ** -0.5)).T
    wsmt_ref[...] = w_t[:IDX_HEADS, :]
    qm_ref[...] = cols(o3 + 3 * LANES, MEM_W)


def _dsa_proj(x2, w_in, kv_norm_g, kidx_g, kidx_b, tk):
    m, d = x2.shape
    o1 = DSA_W
    o2 = o1 + DSA_KV_RANK
    o3 = o2 + IDX_HEADS * IDX_DIM
    o4 = o3 + IDX_DIM
    o5 = o4 + IDX_HEADS
    zeros64 = jnp.zeros((d, IDX_DIM), w_in.dtype)
    w = jnp.concatenate([
        w_in[:, :o3],
        w_in[:, o3:o4], zeros64, zeros64, w_in[:, o3:o4],
        _pad_cols(w_in[:, o4:o5], LANES),
        w_in[:, o5:],
    ], axis=1).astype(BF16)
    z64 = jnp.zeros((IDX_DIM,), F32)
    lng = jnp.concatenate([kidx_g, z64, z64, kidx_g]).reshape(1, 2 * LANES)
    lnb = jnp.concatenate([kidx_b, z64, z64, kidx_b]).reshape(1, 2 * LANES)
    bm = min(ROW_TILE, m)
    nqi = IDX_HEADS * IDX_DIM
    rr = DSA_KV_RANK

    def tok_major(n):
        return pl.BlockSpec((bm, n), lambda i: (i, 0))

    def feat_major(n):
        return pl.BlockSpec((n, bm), lambda i: (0, i))

    return pl.pallas_call(
        functools.partial(_dsa_proj_kernel, tk=tk),
        grid=(m // bm,),
        in_specs=[tok_major(d), _resident(w.shape),
                  _resident((1, rr)), _resident((1, 2 * LANES)), _resident((1, 2 * LANES))],
        out_specs=[feat_major(DSA_W), tok_major(rr),
                   pl.BlockSpec((bm // tk, rr + SUM_ROWS, tk), lambda i: (i, 0, 0)),
                   feat_major(nqi), tok_major(2 * LANES), feat_major(IDX_HEADS), tok_major(MEM_W)],
        out_shape=[jax.ShapeDtypeStruct((DSA_W, m), BF16),
                   jax.ShapeDtypeStruct((m, rr), BF16),
                   jax.ShapeDtypeStruct((m // tk, rr + SUM_ROWS, tk), BF16),
                   jax.ShapeDtypeStruct((nqi, m), BF16),
                   jax.ShapeDtypeStruct((m, 2 * LANES), BF16),
                   jax.ShapeDtypeStruct((IDX_HEADS, m), F32),
                   jax.ShapeDtypeStruct((m, MEM_W), F32)],
        compiler_params=_cparams(("parallel",)),
        name="dsa_proj",
    )(x2, w, kv_norm_g.reshape(1, rr), lng, lnb)


def _ordered_key_to_f32(u):
    key = u ^ jnp.int32(-2 ** 31)
    bits = key ^ ((key >> 31) & jnp.int32(0x7FFFFFFF))
    return lax.bitcast_convert_type(bits, F32)


def _dsa_attn_kernel(qt_ref, qidxt_ref, wsmt_ref, kidx_ref, ckv_ref, ckvt_ref, wukt_ref, wuvt_ref, o_ref,
                     sc_ref, b0_ref, acc_ref, pos_ref, *, tq, tk, topk, pos_bits):
    hh, rr, dh = DSA_HEADS, DSA_KV_RANK, DSA_DH
    t0 = pl.program_id(1) * tq
    nk = (t0 + tq + tk - 1) // tk
    key_i = lax.broadcasted_iota(I32, (tk, tq), 0)
    t_pos = t0 + lax.broadcasted_iota(I32, (tk, tq), 1)
    neg_inf = jnp.float32(-jnp.inf)

    pairs = IDX_HEADS // 2
    lhs_t = jnp.concatenate([qidxt_ref[p * LANES:(p + 1) * LANES, :] for p in range(pairs)], axis=1)
    w_t = wsmt_ref[...]
    w_rows = [jnp.concatenate([w_t[2 * p + half:2 * p + half + 1, :] for p in range(pairs)], axis=1)
              for half in range(2)]

    def idx_body(kb, carry):
        k0 = pl.multiple_of(kb * tk, tk)
        kblk = kidx_ref[0, pl.ds(k0, tk), :]
        acc = jnp.zeros((tk, tq), F32)
        for half in range(2):
            z = jnp.dot(kblk[:, half * LANES:(half + 1) * LANES], lhs_t, preferred_element_type=F32)
            z = jnp.maximum(z, 0.0) * w_rows[half]
            for p in range(pairs):
                acc = acc + z[:, p * tq:(p + 1) * tq]
        sc_ref[kb] = jnp.where(k0 + key_i <= t_pos, acc, neg_inf)
        return carry

    lax.fori_loop(0, nk, idx_body, 0)

    n_valid = (t0 + lax.broadcasted_iota(I32, (1, tq), 1) + 1).astype(F32)
    k_row = jnp.minimum(jnp.float32(topk), n_valid)

    slab_i = lax.broadcasted_iota(I32, (COUNT_ROWS, tq), 0)

    def count(pred):
        def body(kb, acc):
            for j in range(tk // COUNT_ROWS):
                row0 = j * COUNT_ROWS
                slab = sc_ref[kb, row0:row0 + COUNT_ROWS, :]
                acc = acc + jnp.where(pred(slab, kb * tk + row0), jnp.float32(1.0), jnp.float32(0.0))
            return acc
        acc = lax.fori_loop(0, nk, body, jnp.zeros((COUNT_ROWS, tq), F32))
        return jnp.sum(acc, axis=0, keepdims=True)

    def radix_step(it, c):
        tau_u, done = c
        cand_u = tau_u | jnp.left_shift(jnp.int32(1), 31 - it)
        cand = _ordered_key_to_f32(cand_u)
        cnt = count(lambda slab, key0: slab >= cand)
        tau_u = jnp.where((done == 0.0) & (cnt >= k_row), cand_u, tau_u)
        return tau_u, jnp.where(cnt == k_row, 1.0, done)

    def radix_chunk(c):
        it, tau_u, done, _ = c
        tau_u, done = lax.fori_loop(0, RADIX_CHECK, lambda j, s: radix_step(it + j, s), (tau_u, done))
        return it + RADIX_CHECK, tau_u, done, jnp.sum(1.0 - done)

    tau_u, done = lax.fori_loop(0, RADIX_FIXED, radix_step,
                                (jnp.zeros((1, tq), I32), jnp.zeros((1, tq), F32)))
    _, tau_u, _, n_open = lax.while_loop(
        lambda c: (c[0] < 32) & (c[3] > 0.0), radix_chunk,
        (jnp.int32(RADIX_FIXED), tau_u, done, jnp.sum(1.0 - done)))
    tau = _ordered_key_to_f32(tau_u)

    pos_ref[...] = jnp.full((1, tq), NO_INDEX_BOUND, I32)

    @pl.when(n_open > 0.0)
    def _():
        need = k_row - count(lambda slab, key0: slab > tau)

        def pos_body(it, pos):
            cand = pos | jnp.left_shift(jnp.int32(1), pos_bits - 1 - it)
            cnt = count(lambda slab, key0: (slab == tau) & (key0 + slab_i < cand))
            return jnp.where(cnt < need, cand, pos)

        pos_ref[...] = lax.fori_loop(0, pos_bits, pos_body, jnp.zeros((1, tq), I32))

    pos = pos_ref[...]

    qt = qt_ref[...]
    q_lat_t = jnp.concatenate(
        [jnp.dot(wukt_ref[hd], qt[hd * dh:(hd + 1) * dh, :], preferred_element_type=F32)
         for hd in range(hh)], axis=1)
    q_lat_t = (q_lat_t * (dh ** -0.5 * LOG2E)).astype(BF16)
    slopes = [2.0 ** (-8.0 * (hd + 1) / hh) * LOG2E for hd in range(hh)]
    slope_row = jnp.concatenate([jnp.full((1, tq), sl, F32) for sl in slopes], axis=1)
    @pl.when((pl.program_id(0) == 0) & (pl.program_id(1) == 0))
    def _():
        key_f = key_i.astype(F32)
        for hd in range(hh):
            b0_ref[:, hd * tq:(hd + 1) * tq] = key_f * slopes[hd]
    def masked_logits(kb):
        k0 = pl.multiple_of(kb * tk, tk)
        s = jnp.dot(ckv_ref[0, pl.ds(k0, tk), :], q_lat_t, preferred_element_type=F32)
        blk = sc_ref[kb]
        sel = (blk > tau) | ((blk == tau) & (k0 + key_i <= pos))
        bias = jnp.where(sel, 0.0, neg_inf)
        x = s + b0_ref[...] + jnp.concatenate([bias] * hh, axis=1)
        return x, slope_row * (k0 - t0).astype(F32)

    def att_body(kb, m_old):
        x, r = masked_logits(kb)
        m_new = jnp.maximum(m_old, jnp.max(x, axis=0, keepdims=True) + r)
        m_safe = jnp.where(m_new == neg_inf, 0.0, m_new)
        p = jnp.exp2(x - (m_safe - r))
        acc_ref[...] = jnp.exp2(m_old - m_safe) * acc_ref[...] + jnp.dot(
            ckvt_ref[0, kb], p.astype(BF16), preferred_element_type=F32)
        return m_new

    acc_ref[...] = jnp.zeros(acc_ref.shape, F32)
    lax.fori_loop(0, nk, att_body, jnp.full((1, hh * tq), neg_inf, F32))
    o_lat_t = (acc_ref[0:rr, :] / acc_ref[rr:rr + 1, :]).astype(BF16)
    for hd in range(hh):
        o_ref[hd * dh:(hd + 1) * dh, :] = jnp.dot(
            wuvt_ref[hd], o_lat_t[:, hd * tq:(hd + 1) * tq], preferred_element_type=F32).astype(o_ref.dtype)


def _dsa_attn(b, t, qt, qidxt, wsmt, kidx3, ckv3, ckvt4, w_uk, w_uv, tk):
    tq = min(DSA_TQ, t)
    nq = t // tq
    topk = min(TOPK_MAX, t // 4)
    hh, rr = DSA_HEADS, DSA_KV_RANK
    wukt = w_uk.transpose(0, 2, 1).astype(BF16)
    wuvt = w_uv.transpose(0, 2, 1).astype(BF16)

    def qcols(n):
        return pl.BlockSpec((n, tq), lambda bi, ti: (0, bi * nq + ti))

    return pl.pallas_call(
        functools.partial(_dsa_attn_kernel, tq=tq, tk=tk, topk=topk,
                          pos_bits=max(1, (t - 1).bit_length())),
        grid=(b, nq),
        in_specs=[qcols(DSA_W), qcols(IDX_HEADS * IDX_DIM), qcols(IDX_HEADS),
                  pl.BlockSpec((1, t, 2 * LANES), lambda bi, ti: (bi, 0, 0)),
                  pl.BlockSpec((1, t, rr), lambda bi, ti: (bi, 0, 0)),
                  pl.BlockSpec((1, t // tk, rr + SUM_ROWS, tk), lambda bi, ti: (bi, 0, 0, 0)),
                  _resident(wukt.shape), _resident(wuvt.shape)],
        out_specs=qcols(DSA_W),
        out_shape=jax.ShapeDtypeStruct((DSA_W, b * t), BF16),
        scratch_shapes=[pltpu.VMEM((t // tk, tk, tq), F32),
                        pltpu.VMEM((tk, hh * tq), F32),
                        pltpu.VMEM((rr + SUM_ROWS, hh * tq), F32),
                        pltpu.VMEM((1, tq), I32)],
        compiler_params=_cparams(("arbitrary", "arbitrary")),
        name="dsa_attn",
    )(qt, qidxt, wsmt, kidx3, ckv3, ckvt4, wukt, wuvt)


def _mem_kv(mem, w_mem_kv):
    b, mlen, d = mem.shape
    (kv,) = _proj(mem.reshape(b * mlen, d), w_mem_kv.astype(BF16), (2 * MEM_W,), (BF16,))
    return kv.reshape(b, mlen, 2 * MEM_W)


def _gdn_layer(x3, mem, w_in, conv_w, a_log, dt_bias, onorm_g, w_mem_kv, w_o, g, b):
    bsz, t, d = x3.shape
    n_qkv = 3 * GDN_W
    n_gate = 4 * GDN_W
    n_small = n_gate + 2 * GDN_HEADS
    w = jnp.concatenate([w_in[:, :n_gate], _pad_cols(w_in[:, n_gate:n_small], LANES),
                         w_in[:, n_small:]], axis=1).astype(BF16)
    qkv, gate, small, q_mem = _proj(x3.reshape(bsz * t, d), w, (n_qkv, GDN_W, LANES, MEM_W),
                                    (F32, F32, F32, F32))
    o = _gdn_core(qkv.reshape(bsz, t, n_qkv), gate.reshape(bsz, t, GDN_W), small.reshape(bsz, t, LANES),
                  conv_w, a_log, dt_bias, onorm_g)
    return _mix_out(x3, o, q_mem.reshape(bsz, t, MEM_W), _mem_kv(mem, w_mem_kv), w_o, g, b,
                    o_transposed=False)


def _dsa_layer(x3, mem, w_in, kv_norm_g, w_uk, w_uv, kidx_g, kidx_b, w_mem_kv, w_o, g, b):
    bsz, t, d = x3.shape
    tk = min(DSA_TK, t)
    qt, ckv, ckvt, qidxt, kidx, wsmt, q_mem = _dsa_proj(x3.reshape(bsz * t, d), w_in, kv_norm_g,
                                                       kidx_g, kidx_b, tk)
    o_t = _dsa_attn(bsz, t, qt, qidxt, wsmt,
                    kidx.reshape(bsz, t, 2 * LANES), ckv.reshape(bsz, t, DSA_KV_RANK),
                    ckvt.reshape(bsz, t // tk, DSA_KV_RANK + SUM_ROWS, tk), w_uk, w_uv, tk)
    return _mix_out(x3, o_t, q_mem.reshape(bsz, t, MEM_W), _mem_kv(mem, w_mem_kv), w_o, g, b,
                    o_transposed=True)


def kernel(x, mem, l0_ffn1_w_in, l0_ffn1_w_out, l0_ln1_g, l0_ln1_b, l0_w_in, l0_conv_w, l0_a_log, l0_dt_bias, l0_onorm_g, l0_w_mem_kv, l0_w_o, l0_ln2_g, l0_ln2_b, l0_ffn2_w_in, l0_ffn2_w_out, l0_ln3_g, l0_ln3_b, l1_ffn1_w_in, l1_ffn1_w_out, l1_ln1_g, l1_ln1_b, l1_w_in, l1_kv_norm_g, l1_w_uk, l1_w_uv, l1_kidx_ln_g, l1_kidx_ln_b, l1_w_mem_kv, l1_w_o, l1_ln2_g, l1_ln2_b, l1_ffn2_w_in, l1_ffn2_w_out, l1_ln3_g, l1_ln3_b):
    bsz, t, d = x.shape

    def ffn(h3, w_in, w_out, g, b):
        return _ffn_ln(h3.reshape(bsz * t, d), w_in, w_out, g, b).reshape(bsz, t, d)

    h = ffn(x, l0_ffn1_w_in, l0_ffn1_w_out, l0_ln1_g, l0_ln1_b)
    h = _gdn_layer(h, mem, l0_w_in, l0_conv_w, l0_a_log, l0_dt_bias, l0_onorm_g, l0_w_mem_kv, l0_w_o,
                   l0_ln2_g, l0_ln2_b)
    h = ffn(h, l0_ffn2_w_in, l0_ffn2_w_out, l0_ln3_g, l0_ln3_b)
    h = ffn(h, l1_ffn1_w_in, l1_ffn1_w_out, l1_ln1_g, l1_ln1_b)
    h = _dsa_layer(h, mem, l1_w_in, l1_kv_norm_g, l1_w_uk, l1_w_uv, l1_kidx_ln_g, l1_kidx_ln_b,
                   l1_w_mem_kv, l1_w_o, l1_ln2_g, l1_ln2_b)
    h = ffn(h, l1_ffn2_w_in, l1_ffn2_w_out, l1_ln3_g, l1_ln3_b)
    return h
```

```python
import functools
import math

import jax
import jax.numpy as jnp
from jax import lax
from jax.experimental import pallas as pl
from jax.experimental.pallas import tpu as pltpu

F32 = jnp.float32
BF16 = jnp.bfloat16
I32 = jnp.int32

DEPTH = 2
LN_EPS = 1e-5
ALPHA = (2 * DEPTH) ** 0.25
GDN_HEADS = 6
GDN_DK = 128
GDN_W = GDN_HEADS * GDN_DK
GDN_CONV = 4
DSA_HEADS = 12
DSA_DH = 64
DSA_W = DSA_HEADS * DSA_DH
DSA_KV_RANK = 256
IDX_HEADS = 8
IDX_DIM = 64
TOPK_MAX = 256
MEM_HEADS = 4
MEM_DH = 64
MEM_W = MEM_HEADS * MEM_DH

LANES = 128
SUBLANES = 8
VMEM_LIMIT_BYTES = 56 * 1024 * 1024

ROW_TILE = 512
FFN_ROW_TILE = 1024
FFN_SUB_TILES = 2
FFN_CHUNK = 256
GDN_CHUNK = 128
GDN_TILE = 256
INV_BLOCK = 16
DSA_TQ = 2 * LANES
DSA_TK = 512
NO_INDEX_BOUND = 2 ** 30
COUNT_ROWS = 4 * SUBLANES
SUM_ROWS = 2 * SUBLANES
ATT_SLAB = 4 * SUBLANES
RADIX_FIXED = 16
RADIX_CHECK = 4
LOG2E = math.log2(math.e)

def _cparams(sem):
    return pltpu.CompilerParams(dimension_semantics=sem, vmem_limit_bytes=VMEM_LIMIT_BYTES)


def _resident(shape):
    nd = len(shape)
    return pl.BlockSpec(shape, lambda *_: (0,) * nd, pipeline_mode=pl.Buffered(1))


def _mm(a, b):
    return jnp.dot(a.astype(BF16), b.astype(BF16), preferred_element_type=F32)


def _mm_nt(a, b):
    return lax.dot_general(a.astype(BF16), b.astype(BF16), (((1,), (1,)), ((), ())),
                           preferred_element_type=F32)


def _mm_tn(a, b):
    return lax.dot_general(a.astype(BF16), b.astype(BF16), (((0,), (0,)), ((), ())),
                           preferred_element_type=F32)


def _layer_norm(y, g, b):
    mu = jnp.mean(y, axis=-1, keepdims=True)
    d = y - mu
    var = jnp.mean(d * d, axis=-1, keepdims=True)
    return d * lax.rsqrt(var + LN_EPS) * g + b


def _silu(x):
    return x * jax.nn.sigmoid(x)


def _ffn_kernel(x_ref, wg_ref, wu_ref, wo_ref, g_ref, b_ref, o_ref, acc_ref, *, n_chunks, fc, n_sub):
    sub = x_ref.shape[0] // n_sub

    def finish(s):
        rows = slice(s * sub, (s + 1) * sub)
        o_ref[rows, :] = _layer_norm(ALPHA * x_ref[rows, :] + 0.5 * acc_ref[rows, :], g_ref[...], b_ref[...])

    for s in range(n_sub):
        rows = slice(s * sub, (s + 1) * sub)
        xb = x_ref[rows, :].astype(BF16)
        for c in range(n_chunks):
            cs = slice(c * fc, (c + 1) * fc)
            gate = jnp.dot(xb, wg_ref[:, cs], preferred_element_type=F32)
            up = jnp.dot(xb, wu_ref[:, cs], preferred_element_type=F32)
            part = jnp.dot((_silu(gate) * up).astype(BF16), wo_ref[cs, :], preferred_element_type=F32)
            if c == 0:
                acc_ref[rows, :] = part
            else:
                acc_ref[rows, :] += part
            if c == 0 and s > 0:
                finish(s - 1)
    finish(n_sub - 1)


def _prep_ffn(w_in, w_out):
    dff = w_out.shape[0]
    pad = -dff % FFN_CHUNK
    wg = jnp.pad(w_in[:, :dff].astype(BF16), ((0, 0), (0, pad)))
    wu = jnp.pad(w_in[:, dff:].astype(BF16), ((0, 0), (0, pad)))
    wo = jnp.pad(w_out.astype(BF16), ((0, pad), (0, 0)))
    return wg, wu, wo


def _ffn_ln(x2, w_in, w_out, g, b):
    m, d = x2.shape
    wg, wu, wo = _prep_ffn(w_in, w_out)
    bm = min(FFN_ROW_TILE, m)
    row = pl.BlockSpec((bm, d), lambda i: (i, 0))
    return pl.pallas_call(
        functools.partial(_ffn_kernel, n_chunks=wo.shape[0] // FFN_CHUNK, fc=FFN_CHUNK, n_sub=FFN_SUB_TILES),
        grid=(m // bm,),
        in_specs=[row, _resident(wg.shape), _resident(wu.shape), _resident(wo.shape),
                  _resident((1, d)), _resident((1, d))],
        out_specs=row,
        out_shape=jax.ShapeDtypeStruct((m, d), F32),
        scratch_shapes=[pltpu.VMEM((bm, d), F32)],
        compiler_params=_cparams(("parallel",)),
        name="ffn_ln",
    )(x2, wg, wu, wo, g.reshape(1, d), b.reshape(1, d))


def _proj_kernel(x_ref, w_ref, *o_refs, widths):
    xb = x_ref[...].astype(BF16)
    off = 0
    for o_ref, n in zip(o_refs, widths):
        o_ref[...] = jnp.dot(xb, w_ref[:, off:off + n], preferred_element_type=F32).astype(o_ref.dtype)
        off += n


def _proj(x2, w, widths, dtypes):
    m, d = x2.shape
    bm = min(ROW_TILE, m)
    return pl.pallas_call(
        functools.partial(_proj_kernel, widths=widths),
        grid=(m // bm,),
        in_specs=[pl.BlockSpec((bm, d), lambda i: (i, 0)), _resident(w.shape)],
        out_specs=[pl.BlockSpec((bm, n), lambda i: (i, 0)) for n in widths],
        out_shape=[jax.ShapeDtypeStruct((m, n), dt) for n, dt in zip(widths, dtypes)],
        compiler_params=_cparams(("parallel",)),
        name="proj",
    )(x2, w)


def _pad_cols(w, n):
    return jnp.pad(w, ((0, 0), (0, n - w.shape[1])))


def _unit_lower_inverse(mats, xor_idx, eye):
    c = mats[0].shape[0]
    shift = int(math.log2(INV_BLOCK))
    x = [jnp.where((xor_idx >> shift) == 0, a, 0.0) for a in mats]
    p = [eye - xi for xi in x]
    for _ in range(shift - 1):
        x = [_mm(xi, xi) for xi in x]
        p = [pi + _mm(pi, xi) for pi, xi in zip(p, x)]
    t = p
    while (1 << shift) < c:
        off = [jnp.where((xor_idx >> shift) == 1, a, 0.0) for a in mats]
        t_off = [_mm(ti, oi) for ti, oi in zip(t, off)]
        t = [ti - _mm(toi, ti) for ti, toi in zip(t, t_off)]
        shift += 1
    return t


def _gdn_kernel(alog_ref, dtb_ref, q_ref, k_ref, v_ref, gate_ref, small_ref,
                cwq_ref, cwk_ref, cwv_ref, og_ref, o_ref,
                s_ref, halo_ref, stage_ref, smt_ref, *, chunk, n_chunks):
    tc = chunk * n_chunks

    @pl.when(pl.program_id(1) == 0)
    def _():
        s_ref[...] = jnp.zeros_like(s_ref)
        halo_ref[...] = jnp.zeros_like(halo_ref)

    def conv_silu(a, raw_ref, cw_ref):
        raw = raw_ref[0]
        stage_ref[a, 0:SUBLANES, :] = halo_ref[a]
        stage_ref[a, SUBLANES:SUBLANES + tc, :] = raw
        halo_ref[a] = raw[tc - SUBLANES:tc, :]
        cw = cw_ref[...]
        y = cw[GDN_CONV - 1:GDN_CONV, :] * raw
        for j in range(GDN_CONV - 1):
            lo = SUBLANES - (GDN_CONV - 1) + j
            y = y + cw[j:j + 1, :] * stage_ref[a, lo:lo + tc, :]
        return _silu(y)

    q_all = conv_silu(0, q_ref, cwq_ref)
    k_all = conv_silu(1, k_ref, cwk_ref)
    v_all = conv_silu(2, v_ref, cwv_ref)

    smt_ref[...] = small_ref[0].T
    lane_in_chunk = lax.broadcasted_iota(I32, (SUBLANES, tc), 1) & (chunk - 1)

    def to_cols(row):
        return jnp.broadcast_to(row[0:1, :], (LANES, tc)).T

    ri = lax.broadcasted_iota(I32, (chunk, chunk), 0)
    ci = lax.broadcasted_iota(I32, (chunk, chunk), 1)
    incl = ri >= ci
    strict = ri > ci
    xor_idx = ri ^ ci
    eye = jnp.where(ri == ci, 1.0, 0.0).astype(F32)

    def each(f, *lists):
        return [f(*args) for args in zip(*lists)]

    heads = list(range(GDN_HEADS))
    col_of = [slice(h * GDN_DK, (h + 1) * GDN_DK) for h in heads]
    q_h = [q_all[:, cs] for cs in col_of]
    k_h = [k_all[:, cs] for cs in col_of]
    v_h = [v_all[:, cs] for cs in col_of]
    q_h = each(lambda q: q * lax.rsqrt(jnp.sum(q * q, axis=-1, keepdims=True) + 1e-6) * (GDN_DK ** -0.5), q_h)
    k_h = each(lambda k: k * lax.rsqrt(jnp.sum(k * k, axis=-1, keepdims=True) + 1e-6), k_h)
    beta_row = [jax.nn.sigmoid(jnp.broadcast_to(smt_ref[h:h + 1, :], (SUBLANES, tc))) for h in heads]
    g_row = [-jnp.exp(jnp.full((SUBLANES, tc), alog_ref[h], F32))
             * jax.nn.softplus(jnp.broadcast_to(smt_ref[GDN_HEADS + h:GDN_HEADS + h + 1, :], (SUBLANES, tc))
                               + dtb_ref[h]) for h in heads]
    gc_row = g_row
    step = 1
    while step < chunk:
        gc_row = each(lambda g: g + jnp.where(lane_in_chunk >= step, pltpu.roll(g, step, axis=1), 0.0), gc_row)
        step *= 2
    beta_c = each(to_cols, beta_row)
    gc_c = each(to_cols, gc_row)

    probs = [(c, h) for c in range(n_chunks) for h in heads]
    rows_of = [slice(c * chunk, (c + 1) * chunk) for c, _ in probs]
    qc = [q_h[h][rows_of[i]] for i, (_, h) in enumerate(probs)]
    kc = [k_h[h][rows_of[i]] for i, (_, h) in enumerate(probs)]
    vc = [v_h[h][rows_of[i]] for i, (_, h) in enumerate(probs)]
    bc = [beta_c[h][rows_of[i]] for i, (_, h) in enumerate(probs)]
    gcc = [gc_c[h][rows_of[i]] for i, (_, h) in enumerate(probs)]
    gcr = [gc_row[h][0:1, rows_of[i]] for i, (_, h) in enumerate(probs)]
    decay = each(lambda a, b: jnp.where(incl, jnp.exp(jnp.where(incl, a - b, 0.0)), 0.0), gcc, gcr)
    kb = each(lambda a, b: a * b, kc, bc)
    a_mat = each(lambda a, b, d: jnp.where(strict, _mm_nt(a, b) * d, 0.0), kb, kc, decay)
    t_mat = _unit_lower_inverse(a_mat, xor_idx, eye)
    eg = each(jnp.exp, gcc)
    uw = each(lambda t, v, b, kbi, e: _mm(t, jnp.concatenate([v * b, kbi * e], axis=1)), t_mat, vc, bc, kb, eg)
    qk = each(lambda a, b, d: jnp.where(incl, _mm_nt(a, b) * d, 0.0), qc, kc, decay)
    g_last = each(lambda g: g[chunk - 1:chunk, :], gcc)
    k_dec = each(lambda k, gl, g: k * jnp.exp(gl - g), kc, g_last, gcc)
    lhs = each(lambda w, q, e: jnp.concatenate([w[:, GDN_DK:], q * e], axis=0), uw, qc, eg)

    state = [s_ref[h] for h in heads]
    for c in range(n_chunks):
        idx = [c * GDN_HEADS + h for h in heads]
        ws = [_mm(lhs[i], state[h]) for i, h in zip(idx, heads)]
        u = [uw[i][:, :GDN_DK] - w[:chunk] for i, w in zip(idx, ws)]
        o = [w[chunk:] + _mm(qk[i], ui) for i, w, ui in zip(idx, ws, u)]
        state = [state[h] * jnp.exp(g_last[i]) + _mm_tn(k_dec[i], ui) for i, h, ui in zip(idx, heads, u)]
        o = each(lambda x: x * lax.rsqrt(jnp.mean(x * x, axis=-1, keepdims=True) + LN_EPS) * og_ref[...], o)
        for h in heads:
            o_ref[0, rows_of[idx[h]], col_of[h]] = o[h] * _silu(gate_ref[0, rows_of[idx[h]], col_of[h]])
    for h in heads:
        s_ref[h] = state[h]


def _gdn_core(qkv, gate, small, conv_w, a_log, dt_bias, onorm_g):
    b, t, _ = qkv.shape
    tc = min(GDN_TILE, t)
    n_chunks = tc // GDN_CHUNK
    hh = GDN_HEADS

    def tok_block(j):
        return pl.BlockSpec((1, tc, GDN_W), lambda bi, ti: (bi, ti, j))

    def conv_block(j):
        return pl.BlockSpec((GDN_CONV, GDN_W), lambda bi, ti: (0, j))

    smem = pl.BlockSpec(memory_space=pltpu.SMEM)
    return pl.pallas_call(
        functools.partial(_gdn_kernel, chunk=GDN_CHUNK, n_chunks=n_chunks),
        grid=(b, t // tc),
        in_specs=[smem, smem,
                  tok_block(0), tok_block(1), tok_block(2),
                  tok_block(0),
                  pl.BlockSpec((1, tc, LANES), lambda bi, ti: (bi, ti, 0)),
                  conv_block(0), conv_block(1), conv_block(2),
                  pl.BlockSpec((1, GDN_DK), lambda bi, ti: (0, 0))],
        out_specs=tok_block(0),
        out_shape=jax.ShapeDtypeStruct((b, t, GDN_W), F32),
        scratch_shapes=[pltpu.VMEM((hh, GDN_DK, GDN_DK), F32),
                        pltpu.VMEM((3, SUBLANES, GDN_W), F32),
                        pltpu.VMEM((3, SUBLANES + tc, GDN_W), F32),
                        pltpu.VMEM((LANES, tc), F32)],
        compiler_params=_cparams(("parallel", "arbitrary")),
        name="gdn_core",
    )(a_log, dt_bias, qkv, qkv, qkv, gate, small, conv_w, conv_w, conv_w, onorm_g.reshape(1, GDN_DK))


def _mix_out_kernel(x_ref, o_ref, qm_ref, kv_ref, wo1_ref, wo2_ref, g_ref, b_ref, out_ref, *, o_transposed):
    qm = qm_ref[0]
    kv = kv_ref[0]
    k_mem, v_mem = kv[:, :MEM_W], kv[:, MEM_W:]
    head_of_lane = lax.broadcasted_iota(I32, (1, MEM_W), 1) >> (MEM_DH.bit_length() - 1)
    y_mix = _mm_tn(o_ref[...], wo1_ref[...]) if o_transposed else _mm(o_ref[0], wo1_ref[...])
    mine = [head_of_lane == hd for hd in range(MEM_HEADS)]
    s = [_mm_nt(jnp.where(mk, qm, 0.0), k_mem) * (MEM_DH ** -0.5) for mk in mine]
    s = [si - jnp.max(si, axis=-1, keepdims=True) for si in s]
    p = [jnp.exp(si) for si in s]
    p = [pi / jnp.sum(pi, axis=-1, keepdims=True) for pi in p]
    pv = [_mm(pi, v_mem) for pi in p]
    o_mem = jnp.zeros(qm.shape, F32)
    for mk, pvi in zip(mine, pv):
        o_mem = o_mem + jnp.where(mk, pvi, 0.0)
    y = y_mix + _mm(o_mem, wo2_ref[...])
    out_ref[0] = _layer_norm(ALPHA * x_ref[0] + y, g_ref[...], b_ref[...])


def _mix_out(x3, o, qm3, kv3, w_o, g, b, *, o_transposed):
    bsz, t, d = x3.shape
    w1 = o.shape[0] if o_transposed else o.shape[-1]
    bm = min(ROW_TILE, t)
    nt = t // bm
    wo1 = w_o[:w1].astype(BF16)
    wo2 = w_o[w1:].astype(BF16)

    def rows(n):
        return pl.BlockSpec((1, bm, n), lambda bi, ti: (bi, ti, 0))

    o_spec = pl.BlockSpec((w1, bm), lambda bi, ti: (0, bi * nt + ti)) if o_transposed else rows(w1)
    return pl.pallas_call(
        functools.partial(_mix_out_kernel, o_transposed=o_transposed),
        grid=(bsz, nt),
        in_specs=[rows(d), o_spec, rows(MEM_W),
                  pl.BlockSpec((1,) + kv3.shape[1:], lambda bi, ti: (bi, 0, 0)),
                  _resident(wo1.shape), _resident(wo2.shape), _resident((1, d)), _resident((1, d))],
        out_specs=rows(d),
        out_shape=jax.ShapeDtypeStruct((bsz, t, d), F32),
        compiler_params=_cparams(("parallel", "parallel")),
        name="mix_out",
    )(x3, o, qm3, kv3, wo1, wo2, g.reshape(1, d), b.reshape(1, d))


def _dsa_proj_kernel(x_ref, w_ref, kvg_ref, lng_ref, lnb_ref,
                     qt_ref, ckv_ref, ckvt_ref, qidxt_ref, kidx_ref, wsmt_ref, qm_ref, *, tk):
    xb = x_ref[...].astype(BF16)
    bm = xb.shape[0]

    def cols(lo, n):
        return jnp.dot(xb, w_ref[:, lo:lo + n], preferred_element_type=F32)

    o1 = DSA_W
    o2 = o1 + DSA_KV_RANK
    o3 = o2 + IDX_HEADS * IDX_DIM
    qt_ref[...] = cols(0, o1).T.astype(qt_ref.dtype)
    c = cols(o1, DSA_KV_RANK)
    c = c * lax.rsqrt(jnp.mean(c * c, axis=-1, keepdims=True) + LN_EPS) * kvg_ref[...]
    ckv_ref[...] = c.astype(ckv_ref.dtype)
    c_t = c.T.astype(ckvt_ref.dtype)
    for j in range(bm // tk):
        ckvt_ref[j, 0:DSA_KV_RANK, :] = c_t[:, j * tk:(j + 1) * tk]
        ckvt_ref[j, DSA_KV_RANK:, :] = jnp.ones((SUM_ROWS, tk), ckvt_ref.dtype)
    qidxt_ref[...] = cols(o2, IDX_HEADS * IDX_DIM).T.astype(qidxt_ref.dtype)
    lane = lax.broadcasted_iota(I32, (1, LANES), 1)
    for half in range(2):
        kx = cols(o3 + half * LANES, LANES)
        live = (lane >= half * IDX_DIM) & (lane < (half + 1) * IDX_DIM)
        mu = jnp.sum(kx, axis=-1, keepdims=True) * (1.0 / IDX_DIM)
        d = jnp.where(live, kx - mu, 0.0)
        var = jnp.sum(d * d, axis=-1, keepdims=True) * (1.0 / IDX_DIM)
        kn = d * lax.rsqrt(var + LN_EPS) * lng_ref[:, half * LANES:(half + 1) * LANES] \
            + lnb_ref[:, half * LANES:(half + 1) * LANES]
        kidx_ref[:, half * LANES:(half + 1) * LANES] = jnp.where(live, kn, 0.0).astype(kidx_ref.dtype)
    w_t = (cols(o3 + 2 * LANES, LANES) * (IDX_HEADS ** -0.5 * IDX_DIM ** -0.5)).T
    wsmt_ref[...] = w_t[:IDX_HEADS, :]
    qm_ref[...] = cols(o3 + 3 * LANES, MEM_W)


def _dsa_proj(x2, w_in, kv_norm_g, kidx_g, kidx_b, tk):
    m, d = x2.shape
    o1 = DSA_W
    o2 = o1 + DSA_KV_RANK
    o3 = o2 + IDX_HEADS * IDX_DIM
    o4 = o3 + IDX_DIM
    o5 = o4 + IDX_HEADS
    zeros64 = jnp.zeros((d, IDX_DIM), w_in.dtype)
    w = jnp.concatenate([
        w_in[:, :o3],
        w_in[:, o3:o4], zeros64, zeros64, w_in[:, o3:o4],
        _pad_cols(w_in[:, o4:o5], LANES),
        w_in[:, o5:],
    ], axis=1).astype(BF16)
    z64 = jnp.zeros((IDX_DIM,), F32)
    lng = jnp.concatenate([kidx_g, z64, z64, kidx_g]).reshape(1, 2 * LANES)
    lnb = jnp.concatenate([kidx_b, z64, z64, kidx_b]).reshape(1, 2 * LANES)
    bm = min(ROW_TILE, m)
    nqi = IDX_HEADS * IDX_DIM
    rr = DSA_KV_RANK

    def tok_major(n):
        return pl.BlockSpec((bm, n), lambda i: (i, 0))

    def feat_major(n):
        return pl.BlockSpec((n, bm), lambda i: (0, i))

    return pl.pallas_call(
        functools.partial(_dsa_proj_kernel, tk=tk),
        grid=(m // bm,),
        in_specs=[tok_major(d), _resident(w.shape),
                  _resident((1, rr)), _resident((1, 2 * LANES)), _resident((1, 2 * LANES))],
        out_specs=[feat_major(DSA_W), tok_major(rr),
                   pl.BlockSpec((bm // tk, rr + SUM_ROWS, tk), lambda i: (i, 0, 0)),
                   feat_major(nqi), tok_major(2 * LANES), feat_major(IDX_HEADS), tok_major(MEM_W)],
        out_shape=[jax.ShapeDtypeStruct((DSA_W, m), BF16),
                   jax.ShapeDtypeStruct((m, rr), BF16),
                   jax.ShapeDtypeStruct((m // tk, rr + SUM_ROWS, tk), BF16),
                   jax.ShapeDtypeStruct((nqi, m), BF16),
                   jax.ShapeDtypeStruct((m, 2 * LANES), BF16),
                   jax.ShapeDtypeStruct((IDX_HEADS, m), F32),
                   jax.ShapeDtypeStruct((m, MEM_W), F32)],
        compiler_params=_cparams(("parallel",)),
        name="dsa_proj",
    )(x2, w, kv_norm_g.reshape(1, rr), lng, lnb)


def _ordered_key_to_f32(u):
    key = u ^ jnp.int32(-2 ** 31)
    bits = key ^ ((key >> 31) & jnp.int32(0x7FFFFFFF))
    return lax.bitcast_convert_type(bits, F32)


def _dsa_attn_kernel(qt_ref, qidxt_ref, wsmt_ref, kidx_ref, ckv_ref, ckvt_ref, wukt_ref, wuvt_ref, o_ref,
                     sc_ref, b0_ref, acc_ref, qlat_ref, x_ref, p_ref, bias_ref, pos_ref,
                     *, tq, tk, topk, pos_bits):
    hh, rr, dh = DSA_HEADS, DSA_KV_RANK, DSA_DH
    t0 = pl.program_id(1) * tq
    nk = (t0 + tq + tk - 1) // tk
    key_i = lax.broadcasted_iota(I32, (tk, tq), 0)
    t_pos = t0 + lax.broadcasted_iota(I32, (tk, tq), 1)
    neg_inf = jnp.float32(-jnp.inf)

    pairs = IDX_HEADS // 2
    lhs_t = jnp.concatenate([qidxt_ref[p * LANES:(p + 1) * LANES, :] for p in range(pairs)], axis=1)
    w_t = wsmt_ref[...]
    w_rows = [jnp.concatenate([w_t[2 * p + half:2 * p + half + 1, :] for p in range(pairs)], axis=1)
              for half in range(2)]

    def idx_body(kb, carry):
        k0 = pl.multiple_of(kb * tk, tk)
        kblk = kidx_ref[0, pl.ds(k0, tk), :]
        acc = jnp.zeros((tk, tq), F32)
        for half in range(2):
            z = jnp.dot(kblk[:, half * LANES:(half + 1) * LANES], lhs_t, preferred_element_type=F32)
            z = jnp.maximum(z, 0.0) * w_rows[half]
            for p in range(pairs):
                acc = acc + z[:, p * tq:(p + 1) * tq]
        sc_ref[kb] = jnp.where(k0 + key_i <= t_pos, acc, neg_inf)
        return carry

    lax.fori_loop(0, nk, idx_body, 0)

    n_valid = (t0 + lax.broadcasted_iota(I32, (1, tq), 1) + 1).astype(F32)
    k_row = jnp.minimum(jnp.float32(topk), n_valid)

    slab_i = lax.broadcasted_iota(I32, (COUNT_ROWS, tq), 0)

    def count(pred):
        def body(kb, acc):
            for j in range(tk // COUNT_ROWS):
                row0 = j * COUNT_ROWS
                slab = sc_ref[kb, row0:row0 + COUNT_ROWS, :]
                acc = acc + jnp.where(pred(slab, kb * tk + row0), jnp.float32(1.0), jnp.float32(0.0))
            return acc
        acc = lax.fori_loop(0, nk, body, jnp.zeros((COUNT_ROWS, tq), F32))
        return jnp.sum(acc, axis=0, keepdims=True)

    def radix_step(it, c):
        tau_u, done = c
        cand_u = tau_u | jnp.left_shift(jnp.int32(1), 31 - it)
        cand = _ordered_key_to_f32(cand_u)
        cnt = count(lambda slab, key0: slab >= cand)
        tau_u = jnp.where((done == 0.0) & (cnt >= k_row), cand_u, tau_u)
        return tau_u, jnp.where(cnt == k_row, 1.0, done)

    def radix_chunk(c):
        it, tau_u, done, _ = c
        tau_u, done = lax.fori_loop(0, RADIX_CHECK, lambda j, s: radix_step(it + j, s), (tau_u, done))
        return it + RADIX_CHECK, tau_u, done, jnp.sum(1.0 - done)

    tau_u, done = lax.fori_loop(0, RADIX_FIXED, radix_step,
                                (jnp.zeros((1, tq), I32), jnp.zeros((1, tq), F32)))
    _, tau_u, _, n_open = lax.while_loop(
        lambda c: (c[0] < 32) & (c[3] > 0.0), radix_chunk,
        (jnp.int32(RADIX_FIXED), tau_u, done, jnp.sum(1.0 - done)))
    tau = _ordered_key_to_f32(tau_u)

    pos_ref[...] = jnp.full((1, tq), NO_INDEX_BOUND, I32)

    @pl.when(n_open > 0.0)
    def _():
        need = k_row - count(lambda slab, key0: slab > tau)

        def pos_body(it, pos):
            cand = pos | jnp.left_shift(jnp.int32(1), pos_bits - 1 - it)
            cnt = count(lambda slab, key0: (slab == tau) & (key0 + slab_i < cand))
            return jnp.where(cnt < need, cand, pos)

        pos_ref[...] = lax.fori_loop(0, pos_bits, pos_body, jnp.zeros((1, tq), I32))

    pos = pos_ref[...]

    qt = qt_ref[...]
    q_lat_t = jnp.concatenate(
        [jnp.dot(wukt_ref[hd], qt[hd * dh:(hd + 1) * dh, :], preferred_element_type=F32)
         for hd in range(hh)], axis=1)
    qlat_ref[...] = (q_lat_t * (dh ** -0.5 * LOG2E)).astype(BF16)
    slopes = [2.0 ** (-8.0 * (hd + 1) / hh) * LOG2E for hd in range(hh)]
    slope_row = jnp.concatenate([jnp.full((1, tq), sl, F32) for sl in slopes], axis=1)
    @pl.when((pl.program_id(0) == 0) & (pl.program_id(1) == 0))
    def _():
        key_f = key_i.astype(F32)
        for hd in range(hh):
            b0_ref[:, hd * tq:(hd + 1) * tq] = key_f * slopes[hd]
    n_slabs = tk // ATT_SLAB

    def att_body(kb, m_old):
        k0 = pl.multiple_of(kb * tk, tk)
        c_blk = ckv_ref[0, pl.ds(k0, tk), :]
        blk = sc_ref[kb]
        sel = (blk > tau) | ((blk == tau) & (k0 + key_i <= pos))
        bias_ref[...] = jnp.where(sel, 0.0, neg_inf)
        r_all = slope_row * (k0 - t0).astype(F32)
        m_parts, a_parts = [], []
        for hd in range(hh):
            cs = slice(hd * tq, (hd + 1) * tq)
            x_ref[:, cs] = jnp.dot(c_blk, qlat_ref[:, cs], preferred_element_type=F32)
            mx = jnp.full((ATT_SLAB, tq), neg_inf, F32)
            for j in range(n_slabs):
                rows = slice(j * ATT_SLAB, (j + 1) * ATT_SLAB)
                xj = x_ref[rows, cs] + b0_ref[rows, cs] + bias_ref[rows, :]
                x_ref[rows, cs] = xj
                mx = jnp.maximum(mx, xj)
            r = r_all[:, cs]
            m_o = m_old[:, cs]
            m_n = jnp.maximum(m_o, jnp.max(mx, axis=0, keepdims=True) + r)
            m_s = jnp.where(m_n == neg_inf, 0.0, m_n)
            shift = m_s - r
            for j in range(n_slabs):
                rows = slice(j * ATT_SLAB, (j + 1) * ATT_SLAB)
                p_ref[rows, cs] = jnp.exp2(x_ref[rows, cs] - shift).astype(BF16)
            m_parts.append(m_n)
            a_parts.append(jnp.exp2(m_o - m_s))
        acc_ref[...] = jnp.concatenate(a_parts, axis=1) * acc_ref[...] + jnp.dot(
            ckvt_ref[0, kb], p_ref[...], preferred_element_type=F32)
        return jnp.concatenate(m_parts, axis=1)

    acc_ref[...] = jnp.zeros(acc_ref.shape, F32)
    lax.fori_loop(0, nk, att_body, jnp.full((1, hh * tq), neg_inf, F32))
    o_lat_t = (acc_ref[0:rr, :] / acc_ref[rr:rr + 1, :]).astype(BF16)
    for hd in range(hh):
        o_ref[hd * dh:(hd + 1) * dh, :] = jnp.dot(
            wuvt_ref[hd], o_lat_t[:, hd * tq:(hd + 1) * tq], preferred_element_type=F32).astype(o_ref.dtype)


def _dsa_attn(b, t, qt, qidxt, wsmt, kidx3, ckv3, ckvt4, w_uk, w_uv, tk):
    tq = min(DSA_TQ, t)
    nq = t // tq
    topk = min(TOPK_MAX, t // 4)
    hh, rr = DSA_HEADS, DSA_KV_RANK
    wukt = w_uk.transpose(0, 2, 1).astype(BF16)
    wuvt = w_uv.transpose(0, 2, 1).astype(BF16)

    def qcols(n):
        return pl.BlockSpec((n, tq), lambda bi, ti: (0, bi * nq + ti))

    return pl.pallas_call(
        functools.partial(_dsa_attn_kernel, tq=tq, tk=tk, topk=topk,
                          pos_bits=max(1, (t - 1).bit_length())),
        grid=(b, nq),
        in_specs=[qcols(DSA_W), qcols(IDX_HEADS * IDX_DIM), qcols(IDX_HEADS),
                  pl.BlockSpec((1, t, 2 * LANES), lambda bi, ti: (bi, 0, 0)),
                  pl.BlockSpec((1, t, rr), lambda bi, ti: (bi, 0, 0)),
                  pl.BlockSpec((1, t // tk, rr + SUM_ROWS, tk), lambda bi, ti: (bi, 0, 0, 0)),
                  _resident(wukt.shape), _resident(wuvt.shape)],
        out_specs=qcols(DSA_W),
        out_shape=jax.ShapeDtypeStruct((DSA_W, b * t), BF16),
        scratch_shapes=[pltpu.VMEM((t // tk, tk, tq), F32),
                        pltpu.VMEM((tk, hh * tq), F32),
                        pltpu.VMEM((rr + SUM_ROWS, hh * tq), F32),
                        pltpu.VMEM((rr, hh * tq), BF16),
                        pltpu.VMEM((tk, hh * tq), F32),
                        pltpu.VMEM((tk, hh * tq), BF16),
                        pltpu.VMEM((tk, tq), F32),
                        pltpu.VMEM((1, tq), I32)],
        compiler_params=_cparams(("arbitrary", "arbitrary")),
        name="dsa_attn",
    )(qt, qidxt, wsmt, kidx3, ckv3, ckvt4, wukt, wuvt)


def _mem_kv(mem, w_mem_kv):
    b, mlen, d = mem.shape
    (kv,) = _proj(mem.reshape(b * mlen, d), w_mem_kv.astype(BF16), (2 * MEM_W,), (BF16,))
    return kv.reshape(b, mlen, 2 * MEM_W)


def _gdn_layer(x3, mem, w_in, conv_w, a_log, dt_bias, onorm_g, w_mem_kv, w_o, g, b):
    bsz, t, d = x3.shape
    n_qkv = 3 * GDN_W
    n_gate = 4 * GDN_W
    n_small = n_gate + 2 * GDN_HEADS
    w = jnp.concatenate([w_in[:, :n_gate], _pad_cols(w_in[:, n_gate:n_small], LANES),
                         w_in[:, n_small:]], axis=1).astype(BF16)
    qkv, gate, small, q_mem = _proj(x3.reshape(bsz * t, d), w, (n_qkv, GDN_W, LANES, MEM_W),
                                    (F32, F32, F32, F32))
    o = _gdn_core(qkv.reshape(bsz, t, n_qkv), gate.reshape(bsz, t, GDN_W), small.reshape(bsz, t, LANES),
                  conv_w, a_log, dt_bias, onorm_g)
    return _mix_out(x3, o, q_mem.reshape(bsz, t, MEM_W), _mem_kv(mem, w_mem_kv), w_o, g, b,
                    o_transposed=False)


def _dsa_layer(x3, mem, w_in, kv_norm_g, w_uk, w_uv, kidx_g, kidx_b, w_mem_kv, w_o, g, b):
    bsz, t, d = x3.shape
    tk = min(DSA_TK, t)
    qt, ckv, ckvt, qidxt, kidx, wsmt, q_mem = _dsa_proj(x3.reshape(bsz * t, d), w_in, kv_norm_g,
                                                       kidx_g, kidx_b, tk)
    o_t = _dsa_attn(bsz, t, qt, qidxt, wsmt,
                    kidx.reshape(bsz, t, 2 * LANES), ckv.reshape(bsz, t, DSA_KV_RANK),
                    ckvt.reshape(bsz, t // tk, DSA_KV_RANK + SUM_ROWS, tk), w_uk, w_uv, tk)
    return _mix_out(x3, o_t, q_mem.reshape(bsz, t, MEM_W), _mem_kv(mem, w_mem_kv), w_o, g, b,
                    o_transposed=True)


def kernel(x, mem, l0_ffn1_w_in, l0_ffn1_w_out, l0_ln1_g, l0_ln1_b, l0_w_in, l0_conv_w, l0_a_log, l0_dt_bias, l0_onorm_g, l0_w_mem_kv, l0_w_o, l0_ln2_g, l0_ln2_b, l0_ffn2_w_in, l0_ffn2_w_out, l0_ln3_g, l0_ln3_b, l1_ffn1_w_in, l1_ffn1_w_out, l1_ln1_g, l1_ln1_b, l1_w_in, l1_kv_norm_g, l1_w_uk, l1_w_uv, l1_kidx_ln_g, l1_kidx_ln_b, l1_w_mem_kv, l1_w_o, l1_ln2_g, l1_ln2_b, l1_ffn2_w_in, l1_ffn2_w_out, l1_ln3_g, l1_ln3_b):
    bsz, t, d = x.shape

    def ffn(h3, w_in, w_out, g, b):
        return _ffn_ln(h3.reshape(bsz * t, d), w_in, w_out, g, b).reshape(bsz, t, d)

    h = ffn(x, l0_ffn1_w_in, l0_ffn1_w_out, l0_ln1_g, l0_ln1_b)
    h = _gdn_layer(h, mem, l0_w_in, l0_conv_w, l0_a_log, l0_dt_bias, l0_onorm_g, l0_w_mem_kv, l0_w_o,
                   l0_ln2_g, l0_ln2_b)
    h = ffn(h, l0_ffn2_w_in, l0_ffn2_w_out, l0_ln3_g, l0_ln3_b)
    h = ffn(h, l1_ffn1_w_in, l1_ffn1_w_out, l1_ln1_g, l1_ln1_b)
    h = _dsa_layer(h, mem, l1_w_in, l1_kv_norm_g, l1_w_uk, l1_w_uv, l1_kidx_ln_g, l1_kidx_ln_b,
                   l1_w_mem_kv, l1_w_o, l1_ln2_g, l1_ln2_b)
    h = ffn(h, l1_ffn2_w_in, l1_ffn2_w_out, l1_ln3_g, l1_ln3_b)
    return h
```

```python
import functools
import math

import jax
import jax.numpy as jnp
from jax import lax
from jax.experimental import pallas as pl
from jax.experimental.pallas import tpu as pltpu

F32 = jnp.float32
BF16 = jnp.bfloat16
I32 = jnp.int32

DEPTH = 2
LN_EPS = 1e-5
ALPHA = (2 * DEPTH) ** 0.25
GDN_HEADS = 6
GDN_DK = 128
GDN_W = GDN_HEADS * GDN_DK
GDN_CONV = 4
DSA_HEADS = 12
DSA_DH = 64
DSA_W = DSA_HEADS * DSA_DH
DSA_KV_RANK = 256
IDX_HEADS = 8
IDX_DIM = 64
TOPK_MAX = 256
MEM_HEADS = 4
MEM_DH = 64
MEM_W = MEM_HEADS * MEM_DH

LANES = 128
SUBLANES = 8
VMEM_LIMIT_BYTES = 56 * 1024 * 1024

ROW_TILE = 512
FFN_ROW_TILE = 1024
FFN_SUB_TILES = 2
FFN_CHUNK = 256
GDN_CHUNK = 128
GDN_TILE = 256
INV_BLOCK = 16
DSA_TQ = 2 * LANES
DSA_TK = 512
NO_INDEX_BOUND = 2 ** 30
COUNT_ROWS = 4 * SUBLANES
SUM_ROWS = 2 * SUBLANES
ATT_SLAB = 4 * SUBLANES
RADIX_FIXED = 16
RADIX_CHECK = 4
LOG2E = math.log2(math.e)

def _cparams(sem):
    return pltpu.CompilerParams(dimension_semantics=sem, vmem_limit_bytes=VMEM_LIMIT_BYTES)


def _resident(shape):
    nd = len(shape)
    return pl.BlockSpec(shape, lambda *_: (0,) * nd, pipeline_mode=pl.Buffered(1))


def _mm(a, b):
    return jnp.dot(a.astype(BF16), b.astype(BF16), preferred_element_type=F32)


def _mm_nt(a, b):
    return lax.dot_general(a.astype(BF16), b.astype(BF16), (((1,), (1,)), ((), ())),
                           preferred_element_type=F32)


def _mm_tn(a, b):
    return lax.dot_general(a.astype(BF16), b.astype(BF16), (((0,), (0,)), ((), ())),
                           preferred_element_type=F32)


def _layer_norm(y, g, b):
    mu = jnp.mean(y, axis=-1, keepdims=True)
    d = y - mu
    var = jnp.mean(d * d, axis=-1, keepdims=True)
    return d * lax.rsqrt(var + LN_EPS) * g + b


def _silu(x):
    return x * jax.nn.sigmoid(x)


def _ffn_kernel(x_ref, wg_ref, wu_ref, wo_ref, g_ref, b_ref, o_ref, acc_ref, *, n_chunks, fc, n_sub):
    sub = x_ref.shape[0] // n_sub

    def finish(s):
        rows = slice(s * sub, (s + 1) * sub)
        o_ref[rows, :] = _layer_norm(ALPHA * x_ref[rows, :] + 0.5 * acc_ref[rows, :], g_ref[...], b_ref[...])

    for s in range(n_sub):
        rows = slice(s * sub, (s + 1) * sub)
        xb = x_ref[rows, :].astype(BF16)
        for c in range(n_chunks):
            cs = slice(c * fc, (c + 1) * fc)
            gate = jnp.dot(xb, wg_ref[:, cs], preferred_element_type=F32)
            up = jnp.dot(xb, wu_ref[:, cs], preferred_element_type=F32)
            part = jnp.dot((_silu(gate) * up).astype(BF16), wo_ref[cs, :], preferred_element_type=F32)
            if c == 0:
                acc_ref[rows, :] = part
            else:
                acc_ref[rows, :] += part
            if c == 0 and s > 0:
                finish(s - 1)
    finish(n_sub - 1)


def _prep_ffn(w_in, w_out):
    dff = w_out.shape[0]
    pad = -dff % FFN_CHUNK
    wg = jnp.pad(w_in[:, :dff].astype(BF16), ((0, 0), (0, pad)))
    wu = jnp.pad(w_in[:, dff:].astype(BF16), ((0, 0), (0, pad)))
    wo = jnp.pad(w_out.astype(BF16), ((0, pad), (0, 0)))
    return wg, wu, wo


def _ffn_ln(x2, w_in, w_out, g, b):
    m, d = x2.shape
    wg, wu, wo = _prep_ffn(w_in, w_out)
    bm = min(FFN_ROW_TILE, m)
    row = pl.BlockSpec((bm, d), lambda i: (i, 0))
    return pl.pallas_call(
        functools.partial(_ffn_kernel, n_chunks=wo.shape[0] // FFN_CHUNK, fc=FFN_CHUNK, n_sub=FFN_SUB_TILES),
        grid=(m // bm,),
        in_specs=[row, _resident(wg.shape), _resident(wu.shape), _resident(wo.shape),
                  _resident((1, d)), _resident((1, d))],
        out_specs=row,
        out_shape=jax.ShapeDtypeStruct((m, d), F32),
        scratch_shapes=[pltpu.VMEM((bm, d), F32)],
        compiler_params=_cparams(("parallel",)),
        name="ffn_ln",
    )(x2, wg, wu, wo, g.reshape(1, d), b.reshape(1, d))


def _proj_kernel(x_ref, w_ref, *o_refs, widths):
    xb = x_ref[...].astype(BF16)
    off = 0
    for o_ref, n in zip(o_refs, widths):
        o_ref[...] = jnp.dot(xb, w_ref[:, off:off + n], preferred_element_type=F32).astype(o_ref.dtype)
        off += n


def _proj(x2, w, widths, dtypes):
    m, d = x2.shape
    bm = min(ROW_TILE, m)
    return pl.pallas_call(
        functools.partial(_proj_kernel, widths=widths),
        grid=(m // bm,),
        in_specs=[pl.BlockSpec((bm, d), lambda i: (i, 0)), _resident(w.shape)],
        out_specs=[pl.BlockSpec((bm, n), lambda i: (i, 0)) for n in widths],
        out_shape=[jax.ShapeDtypeStruct((m, n), dt) for n, dt in zip(widths, dtypes)],
        compiler_params=_cparams(("parallel",)),
        name="proj",
    )(x2, w)


def _pad_cols(w, n):
    return jnp.pad(w, ((0, 0), (0, n - w.shape[1])))


def _unit_lower_inverse(mats, xor_idx, eye):
    c = mats[0].shape[0]
    shift = int(math.log2(INV_BLOCK))
    x = [jnp.where((xor_idx >> shift) == 0, a, 0.0) for a in mats]
    p = [eye - xi for xi in x]
    for _ in range(shift - 1):
        x = [_mm(xi, xi) for xi in x]
        p = [pi + _mm(pi, xi) for pi, xi in zip(p, x)]
    t = p
    while (1 << shift) < c:
        off = [jnp.where((xor_idx >> shift) == 1, a, 0.0) for a in mats]
        t_off = [_mm(ti, oi) for ti, oi in zip(t, off)]
        t = [ti - _mm(toi, ti) for ti, toi in zip(t, t_off)]
        shift += 1
    return t


def _gdn_kernel(alog_ref, dtb_ref, q_ref, k_ref, v_ref, gate_ref, small_ref,
                cwq_ref, cwk_ref, cwv_ref, og_ref, o_ref,
                s_ref, halo_ref, stage_ref, smt_ref, *, chunk, n_chunks):
    tc = chunk * n_chunks

    @pl.when(pl.program_id(1) == 0)
    def _():
        s_ref[...] = jnp.zeros_like(s_ref)
        halo_ref[...] = jnp.zeros_like(halo_ref)

    def conv_silu(a, raw_ref, cw_ref):
        raw = raw_ref[0]
        stage_ref[a, 0:SUBLANES, :] = halo_ref[a]
        stage_ref[a, SUBLANES:SUBLANES + tc, :] = raw
        halo_ref[a] = raw[tc - SUBLANES:tc, :]
        cw = cw_ref[...]
        y = cw[GDN_CONV - 1:GDN_CONV, :] * raw
        for j in range(GDN_CONV - 1):
            lo = SUBLANES - (GDN_CONV - 1) + j
            y = y + cw[j:j + 1, :] * stage_ref[a, lo:lo + tc, :]
        return _silu(y)

    q_all = conv_silu(0, q_ref, cwq_ref)
    k_all = conv_silu(1, k_ref, cwk_ref)
    v_all = conv_silu(2, v_ref, cwv_ref)

    smt_ref[...] = small_ref[0].T
    lane_in_chunk = lax.broadcasted_iota(I32, (SUBLANES, tc), 1) & (chunk - 1)

    def to_cols(row):
        return jnp.broadcast_to(row[0:1, :], (LANES, tc)).T

    ri = lax.broadcasted_iota(I32, (chunk, chunk), 0)
    ci = lax.broadcasted_iota(I32, (chunk, chunk), 1)
    incl = ri >= ci
    strict = ri > ci
    xor_idx = ri ^ ci
    eye = jnp.where(ri == ci, 1.0, 0.0).astype(F32)

    def each(f, *lists):
        return [f(*args) for args in zip(*lists)]

    heads = list(range(GDN_HEADS))
    col_of = [slice(h * GDN_DK, (h + 1) * GDN_DK) for h in heads]
    q_h = [q_all[:, cs] for cs in col_of]
    k_h = [k_all[:, cs] for cs in col_of]
    v_h = [v_all[:, cs] for cs in col_of]
    q_h = each(lambda q: q * lax.rsqrt(jnp.sum(q * q, axis=-1, keepdims=True) + 1e-6) * (GDN_DK ** -0.5), q_h)
    k_h = each(lambda k: k * lax.rsqrt(jnp.sum(k * k, axis=-1, keepdims=True) + 1e-6), k_h)
    beta_row = [jax.nn.sigmoid(jnp.broadcast_to(smt_ref[h:h + 1, :], (SUBLANES, tc))) for h in heads]
    g_row = [-jnp.exp(jnp.full((SUBLANES, tc), alog_ref[h], F32))
             * jax.nn.softplus(jnp.broadcast_to(smt_ref[GDN_HEADS + h:GDN_HEADS + h + 1, :], (SUBLANES, tc))
                               + dtb_ref[h]) for h in heads]
    gc_row = g_row
    step = 1
    while step < chunk:
        gc_row = each(lambda g: g + jnp.where(lane_in_chunk >= step, pltpu.roll(g, step, axis=1), 0.0), gc_row)
        step *= 2
    beta_c = each(to_cols, beta_row)
    gc_c = each(to_cols, gc_row)

    probs = [(c, h) for c in range(n_chunks) for h in heads]
    rows_of = [slice(c * chunk, (c + 1) * chunk) for c, _ in probs]
    qc = [q_h[h][rows_of[i]] for i, (_, h) in enumerate(probs)]
    kc = [k_h[h][rows_of[i]] for i, (_, h) in enumerate(probs)]
    vc = [v_h[h][rows_of[i]] for i, (_, h) in enumerate(probs)]
    bc = [beta_c[h][rows_of[i]] for i, (_, h) in enumerate(probs)]
    gcc = [gc_c[h][rows_of[i]] for i, (_, h) in enumerate(probs)]
    gcr = [gc_row[h][0:1, rows_of[i]] for i, (_, h) in enumerate(probs)]
    decay = each(lambda a, b: jnp.where(incl, jnp.exp(jnp.where(incl, a - b, 0.0)), 0.0), gcc, gcr)
    kb = each(lambda a, b: a * b, kc, bc)
    a_mat = each(lambda a, b, d: jnp.where(strict, _mm_nt(a, b) * d, 0.0), kb, kc, decay)
    t_mat = _unit_lower_inverse(a_mat, xor_idx, eye)
    eg = each(jnp.exp, gcc)
    uw = each(lambda t, v, b, kbi, e: _mm(t, jnp.concatenate([v * b, kbi * e], axis=1)), t_mat, vc, bc, kb, eg)
    qk = each(lambda a, b, d: jnp.where(incl, _mm_nt(a, b) * d, 0.0), qc, kc, decay)
    g_last = each(lambda g: g[chunk - 1:chunk, :], gcc)
    k_dec = each(lambda k, gl, g: k * jnp.exp(gl - g), kc, g_last, gcc)
    lhs = each(lambda w, q, e: jnp.concatenate([w[:, GDN_DK:], q * e], axis=0), uw, qc, eg)

    state = [s_ref[h] for h in heads]
    for c in range(n_chunks):
        idx = [c * GDN_HEADS + h for h in heads]
        ws = [_mm(lhs[i], state[h]) for i, h in zip(idx, heads)]
        u = [uw[i][:, :GDN_DK] - w[:chunk] for i, w in zip(idx, ws)]
        o = [w[chunk:] + _mm(qk[i], ui) for i, w, ui in zip(idx, ws, u)]
        state = [state[h] * jnp.exp(g_last[i]) + _mm_tn(k_dec[i], ui) for i, h, ui in zip(idx, heads, u)]
        o = each(lambda x: x * lax.rsqrt(jnp.mean(x * x, axis=-1, keepdims=True) + LN_EPS) * og_ref[...], o)
        for h in heads:
            o_ref[0, rows_of[idx[h]], col_of[h]] = o[h] * _silu(gate_ref[0, rows_of[idx[h]], col_of[h]])
    for h in heads:
        s_ref[h] = state[h]


def _gdn_core(qkv, gate, small, conv_w, a_log, dt_bias, onorm_g):
    b, t, _ = qkv.shape
    tc = min(GDN_TILE, t)
    n_chunks = tc // GDN_CHUNK
    hh = GDN_HEADS

    def tok_block(j):
        return pl.BlockSpec((1, tc, GDN_W), lambda bi, ti: (bi, ti, j))

    def conv_block(j):
        return pl.BlockSpec((GDN_CONV, GDN_W), lambda bi, ti: (0, j))

    smem = pl.BlockSpec(memory_space=pltpu.SMEM)
    return pl.pallas_call(
        functools.partial(_gdn_kernel, chunk=GDN_CHUNK, n_chunks=n_chunks),
        grid=(b, t // tc),
        in_specs=[smem, smem,
                  tok_block(0), tok_block(1), tok_block(2),
                  tok_block(0),
                  pl.BlockSpec((1, tc, LANES), lambda bi, ti: (bi, ti, 0)),
                  conv_block(0), conv_block(1), conv_block(2),
                  pl.BlockSpec((1, GDN_DK), lambda bi, ti: (0, 0))],
        out_specs=tok_block(0),
        out_shape=jax.ShapeDtypeStruct((b, t, GDN_W), F32),
        scratch_shapes=[pltpu.VMEM((hh, GDN_DK, GDN_DK), F32),
                        pltpu.VMEM((3, SUBLANES, GDN_W), F32),
                        pltpu.VMEM((3, SUBLANES + tc, GDN_W), F32),
                        pltpu.VMEM((LANES, tc), F32)],
        compiler_params=_cparams(("parallel", "arbitrary")),
        name="gdn_core",
    )(a_log, dt_bias, qkv, qkv, qkv, gate, small, conv_w, conv_w, conv_w, onorm_g.reshape(1, GDN_DK))


def _mix_out_kernel(x_ref, o_ref, qm_ref, kv_ref, wo1_ref, wo2_ref, g_ref, b_ref, out_ref, *, o_transposed):
    qm = qm_ref[0]
    kv = kv_ref[0]
    k_mem, v_mem = kv[:, :MEM_W], kv[:, MEM_W:]
    head_of_lane = lax.broadcasted_iota(I32, (1, MEM_W), 1) >> (MEM_DH.bit_length() - 1)
    y_mix = _mm_tn(o_ref[...], wo1_ref[...]) if o_transposed else _mm(o_ref[0], wo1_ref[...])
    mine = [head_of_lane == hd for hd in range(MEM_HEADS)]
    s = [_mm_nt(jnp.where(mk, qm, 0.0), k_mem) * (MEM_DH ** -0.5) for mk in mine]
    s = [si - jnp.max(si, axis=-1, keepdims=True) for si in s]
    p = [jnp.exp(si) for si in s]
    p = [pi / jnp.sum(pi, axis=-1, keepdims=True) for pi in p]
    pv = [_mm(pi, v_mem) for pi in p]
    o_mem = jnp.zeros(qm.shape, F32)
    for mk, pvi in zip(mine, pv):
        o_mem = o_mem + jnp.where(mk, pvi, 0.0)
    y = y_mix + _mm(o_mem, wo2_ref[...])
    out_ref[0] = _layer_norm(ALPHA * x_ref[0] + y, g_ref[...], b_ref[...])


def _mix_out(x3, o, qm3, kv3, w_o, g, b, *, o_transposed):
    bsz, t, d = x3.shape
    w1 = o.shape[0] if o_transposed else o.shape[-1]
    bm = min(ROW_TILE, t)
    nt = t // bm
    wo1 = w_o[:w1].astype(BF16)
    wo2 = w_o[w1:].astype(BF16)

    def rows(n):
        return pl.BlockSpec((1, bm, n), lambda bi, ti: (bi, ti, 0))

    o_spec = pl.BlockSpec((w1, bm), lambda bi, ti: (0, bi * nt + ti)) if o_transposed else rows(w1)
    return pl.pallas_call(
        functools.partial(_mix_out_kernel, o_transposed=o_transposed),
        grid=(bsz, nt),
        in_specs=[rows(d), o_spec, rows(MEM_W),
                  pl.BlockSpec((1,) + kv3.shape[1:], lambda bi, ti: (bi, 0, 0)),
                  _resident(wo1.shape), _resident(wo2.shape), _resident((1, d)), _resident((1, d))],
        out_specs=rows(d),
        out_shape=jax.ShapeDtypeStruct((bsz, t, d), F32),
        compiler_params=_cparams(("parallel", "parallel")),
        name="mix_out",
    )(x3, o, qm3, kv3, wo1, wo2, g.reshape(1, d), b.reshape(1, d))


def _dsa_proj_kernel(x_ref, w_ref, kvg_ref, lng_ref, lnb_ref,
                     qt_ref, ckv_ref, ckvt_ref, qidxt_ref, kidx_ref, wsmt_ref, qm_ref, *, tk):
    xb = x_ref[...].astype(BF16)
    bm = xb.shape[0]

    def cols(lo, n):
        return jnp.dot(xb, w_ref[:, lo:lo + n], preferred_element_type=F32)

    o1 = DSA_W
    o2 = o1 + DSA_KV_RANK
    o3 = o2 + IDX_HEADS * IDX_DIM
    qt_ref[...] = cols(0, o1).T.astype(qt_ref.dtype)
    c = cols(o1, DSA_KV_RANK)
    c = c * lax.rsqrt(jnp.mean(c * c, axis=-1, keepdims=True) + LN_EPS) * kvg_ref[...]
    ckv_ref[...] = c.astype(ckv_ref.dtype)
    c_t = c.T.astype(ckvt_ref.dtype)
    for j in range(bm // tk):
        ckvt_ref[j, 0:DSA_KV_RANK, :] = c_t[:, j * tk:(j + 1) * tk]
        ckvt_ref[j, DSA_KV_RANK:, :] = jnp.ones((SUM_ROWS, tk), ckvt_ref.dtype)
    qidxt_ref[...] = cols(o2, IDX_HEADS * IDX_DIM).T.astype(qidxt_ref.dtype)
    lane = lax.broadcasted_iota(I32, (1, LANES), 1)
    for half in range(2):
        kx = cols(o3 + half * LANES, LANES)
        live = (lane >= half * IDX_DIM) & (lane < (half + 1) * IDX_DIM)
        mu = jnp.sum(kx, axis=-1, keepdims=True) * (1.0 / IDX_DIM)
        d = jnp.where(live, kx - mu, 0.0)
        var = jnp.sum(d * d, axis=-1, keepdims=True) * (1.0 / IDX_DIM)
        kn = d * lax.rsqrt(var + LN_EPS) * lng_ref[:, half * LANES:(half + 1) * LANES] \
            + lnb_ref[:, half * LANES:(half + 1) * LANES]
        kidx_ref[:, half * LANES:(half + 1) * LANES] = jnp.where(live, kn, 0.0).astype(kidx_ref.dtype)
    w_t = (cols(o3 + 2 * LANES, LANES) * (IDX_HEADS ** -0.5 * IDX_DIM ** -0.5)).T
    wsmt_ref[...] = w_t[:IDX_HEADS, :]
    qm_ref[...] = cols(o3 + 3 * LANES, MEM_W)


def _dsa_proj(x2, w_in, kv_norm_g, kidx_g, kidx_b, tk):
    m, d = x2.shape
    o1 = DSA_W
    o2 = o1 + DSA_KV_RANK
    o3 = o2 + IDX_HEADS * IDX_DIM
    o4 = o3 + IDX_DIM
    o5 = o4 + IDX_HEADS
    zeros64 = jnp.zeros((d, IDX_DIM), w_in.dtype)
    w = jnp.concatenate([
        w_in[:, :o3],
        w_in[:, o3:o4], zeros64, zeros64, w_in[:, o3:o4],
        _pad_cols(w_in[:, o4:o5], LANES),
        w_in[:, o5:],
    ], axis=1).astype(BF16)
    z64 = jnp.zeros((IDX_DIM,), F32)
    lng = jnp.concatenate([kidx_g, z64, z64, kidx_g]).reshape(1, 2 * LANES)
    lnb = jnp.concatenate([kidx_b, z64, z64, kidx_b]).reshape(1, 2 * LANES)
    bm = min(ROW_TILE, m)
    nqi = IDX_HEADS * IDX_DIM
    rr = DSA_KV_RANK

    def tok_major(n):
        return pl.BlockSpec((bm, n), lambda i: (i, 0))

    def feat_major(n):
        return pl.BlockSpec((n, bm), lambda i: (0, i))

    return pl.pallas_call(
        functools.partial(_dsa_proj_kernel, tk=tk),
        grid=(m // bm,),
        in_specs=[tok_major(d), _resident(w.shape),
                  _resident((1, rr)), _resident((1, 2 * LANES)), _resident((1, 2 * LANES))],
        out_specs=[feat_major(DSA_W), tok_major(rr),
                   pl.BlockSpec((bm // tk, rr + SUM_ROWS, tk), lambda i: (i, 0, 0)),
                   feat_major(nqi), tok_major(2 * LANES), feat_major(IDX_HEADS), tok_major(MEM_W)],
        out_shape=[jax.ShapeDtypeStruct((DSA_W, m), BF16),
                   jax.ShapeDtypeStruct((m, rr), BF16),
                   jax.ShapeDtypeStruct((m // tk, rr + SUM_ROWS, tk), BF16),
                   jax.ShapeDtypeStruct((nqi, m), BF16),
                   jax.ShapeDtypeStruct((m, 2 * LANES), BF16),
                   jax.ShapeDtypeStruct((IDX_HEADS, m), F32),
                   jax.ShapeDtypeStruct((m, MEM_W), F32)],
        compiler_params=_cparams(("parallel",)),
        name="dsa_proj",
    )(x2, w, kv_norm_g.reshape(1, rr), lng, lnb)


def _ordered_key_to_f32(u):
    key = u ^ jnp.int32(-2 ** 31)
    bits = key ^ ((key >> 31) & jnp.int32(0x7FFFFFFF))
    return lax.bitcast_convert_type(bits, F32)


def _dsa_attn_kernel(qt_ref, qidxt_ref, wsmt_ref, kidx_ref, ckv_ref, ckvt_ref, wukt_ref, wuvt_ref, o_ref,
                     sc_ref, b0_ref, acc_ref, qlat_ref, x_ref, p_ref, bias_ref, pos_ref,
                     *, tq, tk, topk):
    hh, rr, dh = DSA_HEADS, DSA_KV_RANK, DSA_DH
    t0 = pl.program_id(1) * tq
    nk = (t0 + tq + tk - 1) // tk
    key_i = lax.broadcasted_iota(I32, (tk, tq), 0)
    t_pos = t0 + lax.broadcasted_iota(I32, (tk, tq), 1)
    neg_inf = jnp.float32(-jnp.inf)

    pairs = IDX_HEADS // 2
    lhs_t = jnp.concatenate([qidxt_ref[p * LANES:(p + 1) * LANES, :] for p in range(pairs)], axis=1)
    w_t = wsmt_ref[...]
    w_rows = [jnp.concatenate([w_t[2 * p + half:2 * p + half + 1, :] for p in range(pairs)], axis=1)
              for half in range(2)]

    def idx_body(kb, carry):
        k0 = pl.multiple_of(kb * tk, tk)
        kblk = kidx_ref[0, pl.ds(k0, tk), :]
        acc = jnp.zeros((tk, tq), F32)
        for half in range(2):
            z = jnp.dot(kblk[:, half * LANES:(half + 1) * LANES], lhs_t, preferred_element_type=F32)
            z = jnp.maximum(z, 0.0) * w_rows[half]
            for p in range(pairs):
                acc = acc + z[:, p * tq:(p + 1) * tq]
        sc_ref[kb] = jnp.where(k0 + key_i <= t_pos, acc, neg_inf)
        return carry

    lax.fori_loop(0, nk, idx_body, 0)

    n_valid = (t0 + lax.broadcasted_iota(I32, (1, tq), 1) + 1).astype(F32)
    k_row = jnp.minimum(jnp.float32(topk), n_valid)

    slab_i = lax.broadcasted_iota(I32, (COUNT_ROWS, tq), 0)

    def count(pred):
        def body(kb, acc):
            for j in range(tk // COUNT_ROWS):
                row0 = j * COUNT_ROWS
                slab = sc_ref[kb, row0:row0 + COUNT_ROWS, :]
                acc = acc + jnp.where(pred(slab, kb * tk + row0), jnp.float32(1.0), jnp.float32(0.0))
            return acc
        acc = lax.fori_loop(0, nk, body, jnp.zeros((COUNT_ROWS, tq), F32))
        return jnp.sum(acc, axis=0, keepdims=True)

    def radix_step(it, c):
        tau_u, done = c
        cand_u = tau_u | jnp.left_shift(jnp.int32(1), 31 - it)
        cand = _ordered_key_to_f32(cand_u)
        cnt = count(lambda slab, key0: slab >= cand)
        tau_u = jnp.where((done == 0.0) & (cnt >= k_row), cand_u, tau_u)
        return tau_u, jnp.where(cnt == k_row, 1.0, done)

    def radix_chunk(c):
        it, tau_u, done, _ = c
        tau_u, done = lax.fori_loop(0, RADIX_CHECK, lambda j, s: radix_step(it + j, s), (tau_u, done))
        return it + RADIX_CHECK, tau_u, done, jnp.sum(1.0 - done)

    cnt_pos = count(lambda slab, key0: slab > 0.0)
    cnt_nonneg = count(lambda slab, key0: slab >= 0.0)
    zero_tie = (cnt_pos < k_row) & (cnt_nonneg > k_row)
    tau_u = jnp.where(cnt_nonneg >= k_row, jnp.int32(-2 ** 31), jnp.int32(0))
    done = jnp.where((cnt_nonneg == k_row) | zero_tie, 1.0, 0.0).astype(F32)
    tau_u, done = lax.fori_loop(1, RADIX_FIXED, radix_step, (tau_u, done))
    _, tau_u, _, n_open = lax.while_loop(
        lambda c: (c[0] < 32) & (c[3] > 0.0), radix_chunk,
        (jnp.int32(RADIX_FIXED), tau_u, done, jnp.sum(1.0 - done)))
    tau = _ordered_key_to_f32(tau_u)

    pos_ref[...] = jnp.full((1, tq), NO_INDEX_BOUND, I32)

    @pl.when(n_open + jnp.sum(jnp.where(zero_tie, jnp.float32(1.0), jnp.float32(0.0))) > 0.0)
    def _():
        need = k_row - count(lambda slab, key0: slab > tau)
        ri = lax.broadcasted_iota(I32, (tk, tk), 0)
        ci = lax.broadcasted_iota(I32, (tk, tk), 1)
        lower = jnp.where(ri >= ci, jnp.float32(1.0), jnp.float32(0.0)).astype(BF16)

        def tie_body(kb, c):
            seen, below = c
            ties = jnp.where(sc_ref[kb] == tau, jnp.float32(1.0), jnp.float32(0.0)).astype(BF16)
            running = jnp.dot(lower, ties, preferred_element_type=F32) + seen
            below = below + jnp.sum(jnp.where(running < need, jnp.float32(1.0), jnp.float32(0.0)),
                                    axis=0, keepdims=True)
            return running[tk - 1:tk, :], below

        _, below = lax.fori_loop(0, nk, tie_body, (jnp.zeros((1, tq), F32), jnp.zeros((1, tq), F32)))
        pos_ref[...] = below.astype(I32)

    pos = pos_ref[...]

    qt = qt_ref[...]
    q_lat_t = jnp.concatenate(
        [jnp.dot(wukt_ref[hd], qt[hd * dh:(hd + 1) * dh, :], preferred_element_type=F32)
         for hd in range(hh)], axis=1)
    qlat_ref[...] = (q_lat_t * (dh ** -0.5 * LOG2E)).astype(BF16)
    slopes = [2.0 ** (-8.0 * (hd + 1) / hh) * LOG2E for hd in range(hh)]
    slope_row = jnp.concatenate([jnp.full((1, tq), sl, F32) for sl in slopes], axis=1)
    @pl.when((pl.program_id(0) == 0) & (pl.program_id(1) == 0))
    def _():
        key_f = key_i.astype(F32)
        for hd in range(hh):
            b0_ref[:, hd * tq:(hd + 1) * tq] = key_f * slopes[hd]
    n_slabs = tk // ATT_SLAB

    def att_body(kb, m_old):
        k0 = pl.multiple_of(kb * tk, tk)
        c_blk = ckv_ref[0, pl.ds(k0, tk), :]
        blk = sc_ref[kb]
        sel = (blk > tau) | ((blk == tau) & (k0 + key_i <= pos))
        bias_ref[...] = jnp.where(sel, 0.0, neg_inf)
        r_all = slope_row * (k0 - t0).astype(F32)
        m_parts, a_parts = [], []
        for hd in range(hh):
            cs = slice(hd * tq, (hd + 1) * tq)
            x_ref[:, cs] = jnp.dot(c_blk, qlat_ref[:, cs], preferred_element_type=F32)
            mx = jnp.full((ATT_SLAB, tq), neg_inf, F32)
            for j in range(n_slabs):
                rows = slice(j * ATT_SLAB, (j + 1) * ATT_SLAB)
                xj = x_ref[rows, cs] + b0_ref[rows, cs] + bias_ref[rows, :]
                x_ref[rows, cs] = xj
                mx = jnp.maximum(mx, xj)
            r = r_all[:, cs]
            m_o = m_old[:, cs]
            m_n = jnp.maximum(m_o, jnp.max(mx, axis=0, keepdims=True) + r)
            m_s = jnp.where(m_n == neg_inf, 0.0, m_n)
            shift = m_s - r
            for j in range(n_slabs):
                rows = slice(j * ATT_SLAB, (j + 1) * ATT_SLAB)
                p_ref[rows, cs] = jnp.exp2(x_ref[rows, cs] - shift).astype(BF16)
            m_parts.append(m_n)
            a_parts.append(jnp.exp2(m_o - m_s))
        acc_ref[...] = jnp.concatenate(a_parts, axis=1) * acc_ref[...] + jnp.dot(
            ckvt_ref[0, kb], p_ref[...], preferred_element_type=F32)
        return jnp.concatenate(m_parts, axis=1)

    acc_ref[...] = jnp.zeros(acc_ref.shape, F32)
    lax.fori_loop(0, nk, att_body, jnp.full((1, hh * tq), neg_inf, F32))
    o_lat_t = (acc_ref[0:rr, :] / acc_ref[rr:rr + 1, :]).astype(BF16)
    for hd in range(hh):
        o_ref[hd * dh:(hd + 1) * dh, :] = jnp.dot(
            wuvt_ref[hd], o_lat_t[:, hd * tq:(hd + 1) * tq], preferred_element_type=F32).astype(o_ref.dtype)


def _dsa_attn(b, t, qt, qidxt, wsmt, kidx3, ckv3, ckvt4, w_uk, w_uv, tk):
    tq = min(DSA_TQ, t)
    nq = t // tq
    topk = min(TOPK_MAX, t // 4)
    hh, rr = DSA_HEADS, DSA_KV_RANK
    wukt = w_uk.transpose(0, 2, 1).astype(BF16)
    wuvt = w_uv.transpose(0, 2, 1).astype(BF16)

    def qcols(n):
        return pl.BlockSpec((n, tq), lambda bi, ti: (0, bi * nq + ti))

    return pl.pallas_call(
        functools.partial(_dsa_attn_kernel, tq=tq, tk=tk, topk=topk),
        grid=(b, nq),
        in_specs=[qcols(DSA_W), qcols(IDX_HEADS * IDX_DIM), qcols(IDX_HEADS),
                  pl.BlockSpec((1, t, 2 * LANES), lambda bi, ti: (bi, 0, 0)),
                  pl.BlockSpec((1, t, rr), lambda bi, ti: (bi, 0, 0)),
                  pl.BlockSpec((1, t // tk, rr + SUM_ROWS, tk), lambda bi, ti: (bi, 0, 0, 0)),
                  _resident(wukt.shape), _resident(wuvt.shape)],
        out_specs=qcols(DSA_W),
        out_shape=jax.ShapeDtypeStruct((DSA_W, b * t), BF16),
        scratch_shapes=[pltpu.VMEM((t // tk, tk, tq), F32),
                        pltpu.VMEM((tk, hh * tq), F32),
                        pltpu.VMEM((rr + SUM_ROWS, hh * tq), F32),
                        pltpu.VMEM((rr, hh * tq), BF16),
                        pltpu.VMEM((tk, hh * tq), F32),
                        pltpu.VMEM((tk, hh * tq), BF16),
                        pltpu.VMEM((tk, tq), F32),
                        pltpu.VMEM((1, tq), I32)],
        compiler_params=_cparams(("arbitrary", "arbitrary")),
        name="dsa_attn",
    )(qt, qidxt, wsmt, kidx3, ckv3, ckvt4, wukt, wuvt)


def _mem_kv(mem, w_mem_kv):
    b, mlen, d = mem.shape
    (kv,) = _proj(mem.reshape(b * mlen, d), w_mem_kv.astype(BF16), (2 * MEM_W,), (BF16,))
    return kv.reshape(b, mlen, 2 * MEM_W)


def _gdn_layer(x3, mem, w_in, conv_w, a_log, dt_bias, onorm_g, w_mem_kv, w_o, g, b):
    bsz, t, d = x3.shape
    n_qkv = 3 * GDN_W
    n_gate = 4 * GDN_W
    n_small = n_gate + 2 * GDN_HEADS
    w = jnp.concatenate([w_in[:, :n_gate], _pad_cols(w_in[:, n_gate:n_small], LANES),
                         w_in[:, n_small:]], axis=1).astype(BF16)
    qkv, gate, small, q_mem = _proj(x3.reshape(bsz * t, d), w, (n_qkv, GDN_W, LANES, MEM_W),
                                    (F32, F32, F32, F32))
    o = _gdn_core(qkv.reshape(bsz, t, n_qkv), gate.reshape(bsz, t, GDN_W), small.reshape(bsz, t, LANES),
                  conv_w, a_log, dt_bias, onorm_g)
    return _mix_out(x3, o, q_mem.reshape(bsz, t, MEM_W), _mem_kv(mem, w_mem_kv), w_o, g, b,
                    o_transposed=False)


def _dsa_layer(x3, mem, w_in, kv_norm_g, w_uk, w_uv, kidx_g, kidx_b, w_mem_kv, w_o, g, b):
    bsz, t, d = x3.shape
    tk = min(DSA_TK, t)
    qt, ckv, ckvt, qidxt, kidx, wsmt, q_mem = _dsa_proj(x3.reshape(bsz * t, d), w_in, kv_norm_g,
                                                       kidx_g, kidx_b, tk)
    o_t = _dsa_attn(bsz, t, qt, qidxt, wsmt,
                    kidx.reshape(bsz, t, 2 * LANES), ckv.reshape(bsz, t, DSA_KV_RANK),
                    ckvt.reshape(bsz, t // tk, DSA_KV_RANK + SUM_ROWS, tk), w_uk, w_uv, tk)
    return _mix_out(x3, o_t, q_mem.reshape(bsz, t, MEM_W), _mem_kv(mem, w_mem_kv), w_o, g, b,
                    o_transposed=True)


def kernel(x, mem, l0_ffn1_w_in, l0_ffn1_w_out, l0_ln1_g, l0_ln1_b, l0_w_in, l0_conv_w, l0_a_log, l0_dt_bias, l0_onorm_g, l0_w_mem_kv, l0_w_o, l0_ln2_g, l0_ln2_b, l0_ffn2_w_in, l0_ffn2_w_out, l0_ln3_g, l0_ln3_b, l1_ffn1_w_in, l1_ffn1_w_out, l1_ln1_g, l1_ln1_b, l1_w_in, l1_kv_norm_g, l1_w_uk, l1_w_uv, l1_kidx_ln_g, l1_kidx_ln_b, l1_w_mem_kv, l1_w_o, l1_ln2_g, l1_ln2_b, l1_ffn2_w_in, l1_ffn2_w_out, l1_ln3_g, l1_ln3_b):
    bsz, t, d = x.shape

    def ffn(h3, w_in, w_out, g, b):
        return _ffn_ln(h3.reshape(bsz * t, d), w_in, w_out, g, b).reshape(bsz, t, d)

    h = ffn(x, l0_ffn1_w_in, l0_ffn1_w_out, l0_ln1_g, l0_ln1_b)
    h = _gdn_layer(h, mem, l0_w_in, l0_conv_w, l0_a_log, l0_dt_bias, l0_onorm_g, l0_w_mem_kv, l0_w_o,
                   l0_ln2_g, l0_ln2_b)
    h = ffn(h, l0_ffn2_w_in, l0_ffn2_w_out, l0_ln3_g, l0_ln3_b)
    h = ffn(h, l1_ffn1_w_in, l1_ffn1_w_out, l1_ln1_g, l1_ln1_b)
    h = _dsa_layer(h, mem, l1_w_in, l1_kv_norm_g, l1_w_uk, l1_w_uv, l1_kidx_ln_g, l1_kidx_ln_b,
                   l1_w_mem_kv, l1_w_o, l1_ln2_g, l1_ln2_b)
    h = ffn(h, l1_ffn2_w_in, l1_ffn2_w_out, l1_ln3_g, l1_ln3_b)
    return h
```

```python
import functools
import math

import jax
import jax.numpy as jnp
from jax import lax
from jax.experimental import pallas as pl
from jax.experimental.pallas import tpu as pltpu

F32 = jnp.float32
BF16 = jnp.bfloat16
I32 = jnp.int32

DEPTH = 2
LN_EPS = 1e-5
ALPHA = (2 * DEPTH) ** 0.25
GDN_HEADS = 6
GDN_DK = 128
GDN_W = GDN_HEADS * GDN_DK
GDN_CONV = 4
DSA_HEADS = 12
DSA_DH = 64
DSA_W = DSA_HEADS * DSA_DH
DSA_KV_RANK = 256
IDX_HEADS = 8
IDX_DIM = 64
TOPK_MAX = 256
MEM_HEADS = 4
MEM_DH = 64
MEM_W = MEM_HEADS * MEM_DH

LANES = 128
SUBLANES = 8
VMEM_LIMIT_BYTES = 56 * 1024 * 1024

ROW_TILE = 512
FFN_CHUNK = 256
GDN_CHUNK = 128
GDN_TILE = 256
INV_BLOCK = 16
DSA_TQ = 2 * LANES
DSA_TK = 512
NO_INDEX_BOUND = 2 ** 30
COUNT_ROWS = 4 * SUBLANES
SUM_ROWS = 2 * SUBLANES
ATT_SLAB = 4 * SUBLANES
RADIX_FIXED = 24
RADIX_CHECK = 4
LOG2E = math.log2(math.e)

def _cparams(sem):
    return pltpu.CompilerParams(dimension_semantics=sem, vmem_limit_bytes=VMEM_LIMIT_BYTES)


def _resident(shape):
    nd = len(shape)
    return pl.BlockSpec(shape, lambda *_: (0,) * nd, pipeline_mode=pl.Buffered(1))


def _mm(a, b):
    return jnp.dot(a.astype(BF16), b.astype(BF16), preferred_element_type=F32)


def _mm_nt(a, b):
    return lax.dot_general(a.astype(BF16), b.astype(BF16), (((1,), (1,)), ((), ())),
                           preferred_element_type=F32)


def _mm_tn(a, b):
    return lax.dot_general(a.astype(BF16), b.astype(BF16), (((0,), (0,)), ((), ())),
                           preferred_element_type=F32)


def _layer_norm(y, g, b):
    mu = jnp.mean(y, axis=-1, keepdims=True)
    d = y - mu
    var = jnp.mean(d * d, axis=-1, keepdims=True)
    return d * lax.rsqrt(var + LN_EPS) * g + b


def _silu(x):
    return x * jax.nn.sigmoid(x)


def _ffn_kernel(x_ref, wg_ref, wu_ref, wo_ref, g_ref, b_ref, o_ref, acc_ref, *, n_chunks, fc):
    x = x_ref[...]
    xb = x.astype(BF16)
    for c in range(n_chunks):
        cs = slice(c * fc, (c + 1) * fc)
        gate = jnp.dot(xb, wg_ref[:, cs], preferred_element_type=F32)
        up = jnp.dot(xb, wu_ref[:, cs], preferred_element_type=F32)
        part = jnp.dot((_silu(gate) * up).astype(BF16), wo_ref[cs, :], preferred_element_type=F32)
        if c == 0:
            acc_ref[...] = part
        else:
            acc_ref[...] += part
    o_ref[...] = _layer_norm(ALPHA * x + 0.5 * acc_ref[...], g_ref[...], b_ref[...])


def _prep_ffn(w_in, w_out):
    dff = w_out.shape[0]
    pad = -dff % FFN_CHUNK
    wg = jnp.pad(w_in[:, :dff].astype(BF16), ((0, 0), (0, pad)))
    wu = jnp.pad(w_in[:, dff:].astype(BF16), ((0, 0), (0, pad)))
    wo = jnp.pad(w_out.astype(BF16), ((0, pad), (0, 0)))
    return wg, wu, wo


def _ffn_ln(x2, w_in, w_out, g, b):
    m, d = x2.shape
    wg, wu, wo = _prep_ffn(w_in, w_out)
    bm = min(ROW_TILE, m)
    row = pl.BlockSpec((bm, d), lambda i: (i, 0))
    return pl.pallas_call(
        functools.partial(_ffn_kernel, n_chunks=wo.shape[0] // FFN_CHUNK, fc=FFN_CHUNK),
        grid=(m // bm,),
        in_specs=[row, _resident(wg.shape), _resident(wu.shape), _resident(wo.shape),
                  _resident((1, d)), _resident((1, d))],
        out_specs=row,
        out_shape=jax.ShapeDtypeStruct((m, d), F32),
        scratch_shapes=[pltpu.VMEM((bm, d), F32)],
        compiler_params=_cparams(("parallel",)),
        name="ffn_ln",
    )(x2, wg, wu, wo, g.reshape(1, d), b.reshape(1, d))


def _proj_kernel(x_ref, w_ref, *o_refs, widths):
    xb = x_ref[...].astype(BF16)
    off = 0
    for o_ref, n in zip(o_refs, widths):
        o_ref[...] = jnp.dot(xb, w_ref[:, off:off + n], preferred_element_type=F32).astype(o_ref.dtype)
        off += n


def _proj(x2, w, widths, dtypes):
    m, d = x2.shape
    bm = min(ROW_TILE, m)
    return pl.pallas_call(
        functools.partial(_proj_kernel, widths=widths),
        grid=(m // bm,),
        in_specs=[pl.BlockSpec((bm, d), lambda i: (i, 0)), _resident(w.shape)],
        out_specs=[pl.BlockSpec((bm, n), lambda i: (i, 0)) for n in widths],
        out_shape=[jax.ShapeDtypeStruct((m, n), dt) for n, dt in zip(widths, dtypes)],
        compiler_params=_cparams(("parallel",)),
        name="proj",
    )(x2, w)


def _pad_cols(w, n):
    return jnp.pad(w, ((0, 0), (0, n - w.shape[1])))


def _unit_lower_inverse(mats, xor_idx, eye):
    c = mats[0].shape[0]
    shift = int(math.log2(INV_BLOCK))
    x = [jnp.where((xor_idx >> shift) == 0, a, 0.0) for a in mats]
    p = [eye - xi for xi in x]
    for _ in range(shift - 1):
        x = [_mm(xi, xi) for xi in x]
        p = [pi + _mm(pi, xi) for pi, xi in zip(p, x)]
    t = p
    while (1 << shift) < c:
        off = [jnp.where((xor_idx >> shift) == 1, a, 0.0) for a in mats]
        t_off = [_mm(ti, oi) for ti, oi in zip(t, off)]
        t = [ti - _mm(toi, ti) for ti, toi in zip(t, t_off)]
        shift += 1
    return t


def _gdn_kernel(alog_ref, dtb_ref, q_ref, k_ref, v_ref, gate_ref, small_ref,
                cwq_ref, cwk_ref, cwv_ref, og_ref, o_ref,
                s_ref, halo_ref, stage_ref, smt_ref, *, chunk, n_chunks):
    tc = chunk * n_chunks

    @pl.when(pl.program_id(1) == 0)
    def _():
        s_ref[...] = jnp.zeros_like(s_ref)
        halo_ref[...] = jnp.zeros_like(halo_ref)

    def conv_silu(a, raw_ref, cw_ref):
        raw = raw_ref[0]
        stage_ref[a, 0:SUBLANES, :] = halo_ref[a]
        stage_ref[a, SUBLANES:SUBLANES + tc, :] = raw
        halo_ref[a] = raw[tc - SUBLANES:tc, :]
        cw = cw_ref[...]
        y = cw[GDN_CONV - 1:GDN_CONV, :] * raw
        for j in range(GDN_CONV - 1):
            lo = SUBLANES - (GDN_CONV - 1) + j
            y = y + cw[j:j + 1, :] * stage_ref[a, lo:lo + tc, :]
        return _silu(y)

    q_all = conv_silu(0, q_ref, cwq_ref)
    k_all = conv_silu(1, k_ref, cwk_ref)
    v_all = conv_silu(2, v_ref, cwv_ref)

    smt_ref[...] = small_ref[0].T
    lane_in_chunk = lax.broadcasted_iota(I32, (SUBLANES, tc), 1) & (chunk - 1)

    def to_cols(row):
        return jnp.broadcast_to(row[0:1, :], (LANES, tc)).T

    ri = lax.broadcasted_iota(I32, (chunk, chunk), 0)
    ci = lax.broadcasted_iota(I32, (chunk, chunk), 1)
    incl = ri >= ci
    strict = ri > ci
    xor_idx = ri ^ ci
    eye = jnp.where(ri == ci, 1.0, 0.0).astype(F32)

    def each(f, *lists):
        return [f(*args) for args in zip(*lists)]

    heads = list(range(GDN_HEADS))
    col_of = [slice(h * GDN_DK, (h + 1) * GDN_DK) for h in heads]
    q_h = [q_all[:, cs] for cs in col_of]
    k_h = [k_all[:, cs] for cs in col_of]
    v_h = [v_all[:, cs] for cs in col_of]
    q_h = each(lambda q: q * lax.rsqrt(jnp.sum(q * q, axis=-1, keepdims=True) + 1e-6) * (GDN_DK ** -0.5), q_h)
    k_h = each(lambda k: k * lax.rsqrt(jnp.sum(k * k, axis=-1, keepdims=True) + 1e-6), k_h)
    beta_row = [jax.nn.sigmoid(jnp.broadcast_to(smt_ref[h:h + 1, :], (SUBLANES, tc))) for h in heads]
    g_row = [-jnp.exp(jnp.full((SUBLANES, tc), alog_ref[h], F32))
             * jax.nn.softplus(jnp.broadcast_to(smt_ref[GDN_HEADS + h:GDN_HEADS + h + 1, :], (SUBLANES, tc))
                               + dtb_ref[h]) for h in heads]
    gc_row = g_row
    step = 1
    while step < chunk:
        gc_row = each(lambda g: g + jnp.where(lane_in_chunk >= step, pltpu.roll(g, step, axis=1), 0.0), gc_row)
        step *= 2
    beta_c = each(to_cols, beta_row)
    gc_c = each(to_cols, gc_row)

    probs = [(c, h) for c in range(n_chunks) for h in heads]
    rows_of = [slice(c * chunk, (c + 1) * chunk) for c, _ in probs]
    qc = [q_h[h][rows_of[i]] for i, (_, h) in enumerate(probs)]
    kc = [k_h[h][rows_of[i]] for i, (_, h) in enumerate(probs)]
    vc = [v_h[h][rows_of[i]] for i, (_, h) in enumerate(probs)]
    bc = [beta_c[h][rows_of[i]] for i, (_, h) in enumerate(probs)]
    gcc = [gc_c[h][rows_of[i]] for i, (_, h) in enumerate(probs)]
    gcr = [gc_row[h][0:1, rows_of[i]] for i, (_, h) in enumerate(probs)]
    decay = each(lambda a, b: jnp.where(incl, jnp.exp(jnp.where(incl, a - b, 0.0)), 0.0), gcc, gcr)
    kb = each(lambda a, b: a * b, kc, bc)
    a_mat = each(lambda a, b, d: jnp.where(strict, _mm_nt(a, b) * d, 0.0), kb, kc, decay)
    t_mat = _unit_lower_inverse(a_mat, xor_idx, eye)
    eg = each(jnp.exp, gcc)
    uw = each(lambda t, v, b, kbi, e: _mm(t, jnp.concatenate([v * b, kbi * e], axis=1)), t_mat, vc, bc, kb, eg)
    qk = each(lambda a, b, d: jnp.where(incl, _mm_nt(a, b) * d, 0.0), qc, kc, decay)
    g_last = each(lambda g: g[chunk - 1:chunk, :], gcc)
    k_dec = each(lambda k, gl, g: k * jnp.exp(gl - g), kc, g_last, gcc)
    lhs = each(lambda w, q, e: jnp.concatenate([w[:, GDN_DK:], q * e], axis=0), uw, qc, eg)

    state = [s_ref[h] for h in heads]
    for c in range(n_chunks):
        idx = [c * GDN_HEADS + h for h in heads]
        ws = [_mm(lhs[i], state[h]) for i, h in zip(idx, heads)]
        u = [uw[i][:, :GDN_DK] - w[:chunk] for i, w in zip(idx, ws)]
        o = [w[chunk:] + _mm(qk[i], ui) for i, w, ui in zip(idx, ws, u)]
        state = [state[h] * jnp.exp(g_last[i]) + _mm_tn(k_dec[i], ui) for i, h, ui in zip(idx, heads, u)]
        o = each(lambda x: x * lax.rsqrt(jnp.mean(x * x, axis=-1, keepdims=True) + LN_EPS) * og_ref[...], o)
        for h in heads:
            o_ref[0, rows_of[idx[h]], col_of[h]] = o[h] * _silu(gate_ref[0, rows_of[idx[h]], col_of[h]])
    for h in heads:
        s_ref[h] = state[h]


def _gdn_core(qkv, gate, small, conv_w, a_log, dt_bias, onorm_g):
    b, t, _ = qkv.shape
    tc = min(GDN_TILE, t)
    n_chunks = tc // GDN_CHUNK
    hh = GDN_HEADS

    def tok_block(j):
        return pl.BlockSpec((1, tc, GDN_W), lambda bi, ti: (bi, ti, j))

    def conv_block(j):
        return pl.BlockSpec((GDN_CONV, GDN_W), lambda bi, ti: (0, j))

    smem = pl.BlockSpec(memory_space=pltpu.SMEM)
    return pl.pallas_call(
        functools.partial(_gdn_kernel, chunk=GDN_CHUNK, n_chunks=n_chunks),
        grid=(b, t // tc),
        in_specs=[smem, smem,
                  tok_block(0), tok_block(1), tok_block(2),
                  tok_block(0),
                  pl.BlockSpec((1, tc, LANES), lambda bi, ti: (bi, ti, 0)),
                  conv_block(0), conv_block(1), conv_block(2),
                  pl.BlockSpec((1, GDN_DK), lambda bi, ti: (0, 0))],
        out_specs=tok_block(0),
        out_shape=jax.ShapeDtypeStruct((b, t, GDN_W), F32),
        scratch_shapes=[pltpu.VMEM((hh, GDN_DK, GDN_DK), F32),
                        pltpu.VMEM((3, SUBLANES, GDN_W), F32),
                        pltpu.VMEM((3, SUBLANES + tc, GDN_W), F32),
                        pltpu.VMEM((LANES, tc), F32)],
        compiler_params=_cparams(("parallel", "arbitrary")),
        name="gdn_core",
    )(a_log, dt_bias, qkv, qkv, qkv, gate, small, conv_w, conv_w, conv_w, onorm_g.reshape(1, GDN_DK))


def _mix_out_kernel(x_ref, o_ref, qm_ref, kv_ref, wo1_ref, wo2_ref, g_ref, b_ref, out_ref, *, o_transposed):
    qm = qm_ref[0]
    kv = kv_ref[0]
    k_mem, v_mem = kv[:, :MEM_W], kv[:, MEM_W:]
    head_of_lane = lax.broadcasted_iota(I32, (1, MEM_W), 1) >> (MEM_DH.bit_length() - 1)
    y_mix = _mm_tn(o_ref[...], wo1_ref[...]) if o_transposed else _mm(o_ref[0], wo1_ref[...])
    mine = [head_of_lane == hd for hd in range(MEM_HEADS)]
    s = [_mm_nt(jnp.where(mk, qm, 0.0), k_mem) * (MEM_DH ** -0.5) for mk in mine]
    s = [si - jnp.max(si, axis=-1, keepdims=True) for si in s]
    p = [jnp.exp(si) for si in s]
    p = [pi / jnp.sum(pi, axis=-1, keepdims=True) for pi in p]
    pv = [_mm(pi, v_mem) for pi in p]
    o_mem = jnp.zeros(qm.shape, F32)
    for mk, pvi in zip(mine, pv):
        o_mem = o_mem + jnp.where(mk, pvi, 0.0)
    y = y_mix + _mm(o_mem, wo2_ref[...])
    out_ref[0] = _layer_norm(ALPHA * x_ref[0] + y, g_ref[...], b_ref[...])


def _mix_out(x3, o, qm3, kv3, w_o, g, b, *, o_transposed):
    bsz, t, d = x3.shape
    w1 = o.shape[0] if o_transposed else o.shape[-1]
    bm = min(ROW_TILE, t)
    nt = t // bm
    wo1 = w_o[:w1].astype(BF16)
    wo2 = w_o[w1:].astype(BF16)

    def rows(n):
        return pl.BlockSpec((1, bm, n), lambda bi, ti: (bi, ti, 0))

    o_spec = pl.BlockSpec((w1, bm), lambda bi, ti: (0, bi * nt + ti)) if o_transposed else rows(w1)
    return pl.pallas_call(
        functools.partial(_mix_out_kernel, o_transposed=o_transposed),
        grid=(bsz, nt),
        in_specs=[rows(d), o_spec, rows(MEM_W),
                  pl.BlockSpec((1,) + kv3.shape[1:], lambda bi, ti: (bi, 0, 0)),
                  _resident(wo1.shape), _resident(wo2.shape), _resident((1, d)), _resident((1, d))],
        out_specs=rows(d),
        out_shape=jax.ShapeDtypeStruct((bsz, t, d), F32),
        compiler_params=_cparams(("parallel", "parallel")),
        name="mix_out",
    )(x3, o, qm3, kv3, wo1, wo2, g.reshape(1, d), b.reshape(1, d))


def _dsa_proj_kernel(x_ref, w_ref, kvg_ref, lng_ref, lnb_ref,
                     qt_ref, ckv_ref, ckvt_ref, qidxt_ref, kidx_ref, wsmt_ref, qm_ref, *, tk):
    xb = x_ref[...].astype(BF16)
    bm = xb.shape[0]

    def cols(lo, n):
        return jnp.dot(xb, w_ref[:, lo:lo + n], preferred_element_type=F32)

    o1 = DSA_W
    o2 = o1 + DSA_KV_RANK
    o3 = o2 + IDX_HEADS * IDX_DIM
    qt_ref[...] = cols(0, o1).T.astype(qt_ref.dtype)
    c = cols(o1, DSA_KV_RANK)
    c = c * lax.rsqrt(jnp.mean(c * c, axis=-1, keepdims=True) + LN_EPS) * kvg_ref[...]
    ckv_ref[...] = c.astype(ckv_ref.dtype)
    c_t = c.T.astype(ckvt_ref.dtype)
    for j in range(bm // tk):
        ckvt_ref[j, 0:DSA_KV_RANK, :] = c_t[:, j * tk:(j + 1) * tk]
        ckvt_ref[j, DSA_KV_RANK:, :] = jnp.ones((SUM_ROWS, tk), ckvt_ref.dtype)
    qidxt_ref[...] = cols(o2, IDX_HEADS * IDX_DIM).T.astype(qidxt_ref.dtype)
    lane = lax.broadcasted_iota(I32, (1, LANES), 1)
    for half in range(2):
        kx = cols(o3 + half * LANES, LANES)
        live = (lane >= half * IDX_DIM) & (lane < (half + 1) * IDX_DIM)
        mu = jnp.sum(kx, axis=-1, keepdims=True) * (1.0 / IDX_DIM)
        d = jnp.where(live, kx - mu, 0.0)
        var = jnp.sum(d * d, axis=-1, keepdims=True) * (1.0 / IDX_DIM)
        kn = d * lax.rsqrt(var + LN_EPS) * lng_ref[:, half * LANES:(half + 1) * LANES] \
            + lnb_ref[:, half * LANES:(half + 1) * LANES]
        kidx_ref[:, half * LANES:(half + 1) * LANES] = jnp.where(live, kn, 0.0).astype(kidx_ref.dtype)
    w_t = (cols(o3 + 2 * LANES, LANES) * (IDX_HEADS ** -0.5 * IDX_DIM ** -0.5)).T
    wsmt_ref[...] = w_t[:IDX_HEADS, :]
    qm_ref[...] = cols(o3 + 3 * LANES, MEM_W)


def _dsa_proj(x2, w_in, kv_norm_g, kidx_g, kidx_b, tk):
    m, d = x2.shape
    o1 = DSA_W
    o2 = o1 + DSA_KV_RANK
    o3 = o2 + IDX_HEADS * IDX_DIM
    o4 = o3 + IDX_DIM
    o5 = o4 + IDX_HEADS
    zeros64 = jnp.zeros((d, IDX_DIM), w_in.dtype)
    w = jnp.concatenate([
        w_in[:, :o3],
        w_in[:, o3:o4], zeros64, zeros64, w_in[:, o3:o4],
        _pad_cols(w_in[:, o4:o5], LANES),
        w_in[:, o5:],
    ], axis=1).astype(BF16)
    z64 = jnp.zeros((IDX_DIM,), F32)
    lng = jnp.concatenate([kidx_g, z64, z64, kidx_g]).reshape(1, 2 * LANES)
    lnb = jnp.concatenate([kidx_b, z64, z64, kidx_b]).reshape(1, 2 * LANES)
    bm = min(ROW_TILE, m)
    nqi = IDX_HEADS * IDX_DIM
    rr = DSA_KV_RANK

    def tok_major(n):
        return pl.BlockSpec((bm, n), lambda i: (i, 0))

    def feat_major(n):
        return pl.BlockSpec((n, bm), lambda i: (0, i))

    return pl.pallas_call(
        functools.partial(_dsa_proj_kernel, tk=tk),
        grid=(m // bm,),
        in_specs=[tok_major(d), _resident(w.shape),
                  _resident((1, rr)), _resident((1, 2 * LANES)), _resident((1, 2 * LANES))],
        out_specs=[feat_major(DSA_W), tok_major(rr),
                   pl.BlockSpec((bm // tk, rr + SUM_ROWS, tk), lambda i: (i, 0, 0)),
                   feat_major(nqi), tok_major(2 * LANES), feat_major(IDX_HEADS), tok_major(MEM_W)],
        out_shape=[jax.ShapeDtypeStruct((DSA_W, m), BF16),
                   jax.ShapeDtypeStruct((m, rr), BF16),
                   jax.ShapeDtypeStruct((m // tk, rr + SUM_ROWS, tk), BF16),
                   jax.ShapeDtypeStruct((nqi, m), BF16),
                   jax.ShapeDtypeStruct((m, 2 * LANES), BF16),
                   jax.ShapeDtypeStruct((IDX_HEADS, m), F32),
                   jax.ShapeDtypeStruct((m, MEM_W), F32)],
        compiler_params=_cparams(("parallel",)),
        name="dsa_proj",
    )(x2, w, kv_norm_g.reshape(1, rr), lng, lnb)


def _ordered_key_to_f32(u):
    key = u ^ jnp.int32(-2 ** 31)
    bits = key ^ ((key >> 31) & jnp.int32(0x7FFFFFFF))
    return lax.bitcast_convert_type(bits, F32)


def _dsa_attn_kernel(qt_ref, qidxt_ref, wsmt_ref, kidx_ref, ckv_ref, ckvt_ref, wukt_ref, wuvt_ref, o_ref,
                     sc_ref, b0_ref, acc_ref, qlat_ref, x_ref, p_ref, bias_ref, pos_ref,
                     *, tq, tk, topk):
    hh, rr, dh = DSA_HEADS, DSA_KV_RANK, DSA_DH
    t0 = pl.program_id(1) * tq
    nk = (t0 + tq + tk - 1) // tk
    key_i = lax.broadcasted_iota(I32, (tk, tq), 0)
    t_pos = t0 + lax.broadcasted_iota(I32, (tk, tq), 1)
    neg_inf = jnp.float32(-jnp.inf)

    pairs = IDX_HEADS // 2
    lhs_t = jnp.concatenate([qidxt_ref[p * LANES:(p + 1) * LANES, :] for p in range(pairs)], axis=1)
    w_t = wsmt_ref[...]
    w_rows = [jnp.concatenate([w_t[2 * p + half:2 * p + half + 1, :] for p in range(pairs)], axis=1)
              for half in range(2)]

    def idx_body(kb, carry):
        k0 = pl.multiple_of(kb * tk, tk)
        kblk = kidx_ref[0, pl.ds(k0, tk), :]
        acc = jnp.zeros((tk, tq), F32)
        for half in range(2):
            z = jnp.dot(kblk[:, half * LANES:(half + 1) * LANES], lhs_t, preferred_element_type=F32)
            z = jnp.maximum(z, 0.0) * w_rows[half]
            for p in range(pairs):
                acc = acc + z[:, p * tq:(p + 1) * tq]
        sc_ref[kb] = jnp.where(k0 + key_i <= t_pos, acc, neg_inf)
        return carry

    lax.fori_loop(0, nk, idx_body, 0)

    n_valid = (t0 + lax.broadcasted_iota(I32, (1, tq), 1) + 1).astype(F32)
    k_row = jnp.minimum(jnp.float32(topk), n_valid)

    slab_i = lax.broadcasted_iota(I32, (COUNT_ROWS, tq), 0)

    def count(pred):
        def body(kb, acc):
            for j in range(tk // COUNT_ROWS):
                row0 = j * COUNT_ROWS
                slab = sc_ref[kb, row0:row0 + COUNT_ROWS, :]
                acc = acc + jnp.where(pred(slab, kb * tk + row0), jnp.float32(1.0), jnp.float32(0.0))
            return acc
        acc = lax.fori_loop(0, nk, body, jnp.zeros((COUNT_ROWS, tq), F32))
        return jnp.sum(acc, axis=0, keepdims=True)

    def radix_step(it, c):
        tau_u, done = c
        cand_u = tau_u | jnp.left_shift(jnp.int32(1), 31 - it)
        cand = _ordered_key_to_f32(cand_u)
        cnt = count(lambda slab, key0: slab >= cand)
        tau_u = jnp.where((done == 0.0) & (cnt >= k_row), cand_u, tau_u)
        return tau_u, jnp.where(cnt == k_row, 1.0, done)

    def radix_chunk(c):
        it, tau_u, done, _ = c
        tau_u, done = lax.fori_loop(0, RADIX_CHECK, lambda j, s: radix_step(it + j, s), (tau_u, done))
        return it + RADIX_CHECK, tau_u, done, jnp.sum(1.0 - done)

    cnt_pos = count(lambda slab, key0: slab > 0.0)
    cnt_nonneg = count(lambda slab, key0: slab >= 0.0)
    zero_tie = (cnt_pos < k_row) & (cnt_nonneg > k_row)
    tau_u = jnp.where(cnt_nonneg >= k_row, jnp.int32(-2 ** 31), jnp.int32(0))
    done = jnp.where((cnt_nonneg == k_row) | zero_tie, 1.0, 0.0).astype(F32)
    tau_u, done = lax.fori_loop(1, RADIX_FIXED, radix_step, (tau_u, done))
    _, tau_u, _, n_open = lax.while_loop(
        lambda c: (c[0] < 32) & (c[3] > 0.0), radix_chunk,
        (jnp.int32(RADIX_FIXED), tau_u, done, jnp.sum(1.0 - done)))
    tau = _ordered_key_to_f32(tau_u)

    pos_ref[...] = jnp.full((1, tq), NO_INDEX_BOUND, I32)

    @pl.when(n_open + jnp.sum(jnp.where(zero_tie, jnp.float32(1.0), jnp.float32(0.0))) > 0.0)
    def _():
        need = k_row - count(lambda slab, key0: slab > tau)
        ri = lax.broadcasted_iota(I32, (tk, tk), 0)
        ci = lax.broadcasted_iota(I32, (tk, tk), 1)
        lower = jnp.where(ri >= ci, jnp.float32(1.0), jnp.float32(0.0)).astype(BF16)

        def tie_body(kb, c):
            seen, below = c
            ties = jnp.where(sc_ref[kb] == tau, jnp.float32(1.0), jnp.float32(0.0)).astype(BF16)
            running = jnp.dot(lower, ties, preferred_element_type=F32) + seen
            below = below + jnp.sum(jnp.where(running < need, jnp.float32(1.0), jnp.float32(0.0)),
                                    axis=0, keepdims=True)
            return running[tk - 1:tk, :], below

        _, below = lax.fori_loop(0, nk, tie_body, (jnp.zeros((1, tq), F32), jnp.zeros((1, tq), F32)))
        pos_ref[...] = below.astype(I32)

    pos = pos_ref[...]

    qt = qt_ref[...]
    q_lat_t = jnp.concatenate(
        [jnp.dot(wukt_ref[hd], qt[hd * dh:(hd + 1) * dh, :], preferred_element_type=F32)
         for hd in range(hh)], axis=1)
    qlat_ref[...] = (q_lat_t * (dh ** -0.5 * LOG2E)).astype(BF16)
    slopes = [2.0 ** (-8.0 * (hd + 1) / hh) * LOG2E for hd in range(hh)]
    slope_row = jnp.concatenate([jnp.full((1, tq), sl, F32) for sl in slopes], axis=1)
    @pl.when((pl.program_id(0) == 0) & (pl.program_id(1) == 0))
    def _():
        key_f = key_i.astype(F32)
        for hd in range(hh):
            b0_ref[:, hd * tq:(hd + 1) * tq] = key_f * slopes[hd]
    n_slabs = tk // ATT_SLAB

    def att_body(kb, m_old):
        k0 = pl.multiple_of(kb * tk, tk)
        c_blk = ckv_ref[0, pl.ds(k0, tk), :]
        blk = sc_ref[kb]
        sel = (blk > tau) | ((blk == tau) & (k0 + key_i <= pos))
        bias_ref[...] = jnp.where(sel, 0.0, neg_inf)
        r_all = slope_row * (k0 - t0).astype(F32)
        m_parts, a_parts = [], []
        for hd in range(hh):
            cs = slice(hd * tq, (hd + 1) * tq)
            x_ref[:, cs] = jnp.dot(c_blk, qlat_ref[:, cs], preferred_element_type=F32)
            mx = jnp.full((ATT_SLAB, tq), neg_inf, F32)
            for j in range(n_slabs):
                rows = slice(j * ATT_SLAB, (j + 1) * ATT_SLAB)
                xj = x_ref[rows, cs] + b0_ref[rows, cs] + bias_ref[rows, :]
                x_ref[rows, cs] = xj
                mx = jnp.maximum(mx, xj)
            r = r_all[:, cs]
            m_o = m_old[:, cs]
            m_n = jnp.maximum(m_o, jnp.max(mx, axis=0, keepdims=True) + r)
            m_s = jnp.where(m_n == neg_inf, 0.0, m_n)
            shift = m_s - r
            for j in range(n_slabs):
                rows = slice(j * ATT_SLAB, (j + 1) * ATT_SLAB)
                p_ref[rows, cs] = jnp.exp2(x_ref[rows, cs] - shift).astype(BF16)
            m_parts.append(m_n)
            a_parts.append(jnp.exp2(m_o - m_s))
        acc_ref[...] = jnp.concatenate(a_parts, axis=1) * acc_ref[...] + jnp.dot(
            ckvt_ref[0, kb], p_ref[...], preferred_element_type=F32)
        return jnp.concatenate(m_parts, axis=1)

    acc_ref[...] = jnp.zeros(acc_ref.shape, F32)
    lax.fori_loop(0, nk, att_body, jnp.full((1, hh * tq), neg_inf, F32))
    o_lat_t = (acc_ref[0:rr, :] / acc_ref[rr:rr + 1, :]).astype(BF16)
    for hd in range(hh):
        o_ref[hd * dh:(hd + 1) * dh, :] = jnp.dot(
            wuvt_ref[hd], o_lat_t[:, hd * tq:(hd + 1) * tq], preferred_element_type=F32).astype(o_ref.dtype)


def _dsa_attn(b, t, qt, qidxt, wsmt, kidx3, ckv3, ckvt4, w_uk, w_uv, tk):
    tq = min(DSA_TQ, t)
    nq = t // tq
    topk = min(TOPK_MAX, t // 4)
    hh, rr = DSA_HEADS, DSA_KV_RANK
    wukt = w_uk.transpose(0, 2, 1).astype(BF16)
    wuvt = w_uv.transpose(0, 2, 1).astype(BF16)

    def qcols(n):
        return pl.BlockSpec((n, tq), lambda bi, ti: (0, bi * nq + ti))

    return pl.pallas_call(
        functools.partial(_dsa_attn_kernel, tq=tq, tk=tk, topk=topk),
        grid=(b, nq),
        in_specs=[qcols(DSA_W), qcols(IDX_HEADS * IDX_DIM), qcols(IDX_HEADS),
                  pl.BlockSpec((1, t, 2 * LANES), lambda bi, ti: (bi, 0, 0)),
                  pl.BlockSpec((1, t, rr), lambda bi, ti: (bi, 0, 0)),
                  pl.BlockSpec((1, t // tk, rr + SUM_ROWS, tk), lambda bi, ti: (bi, 0, 0, 0)),
                  _resident(wukt.shape), _resident(wuvt.shape)],
        out_specs=qcols(DSA_W),
        out_shape=jax.ShapeDtypeStruct((DSA_W, b * t), BF16),
        scratch_shapes=[pltpu.VMEM((t // tk, tk, tq), F32),
                        pltpu.VMEM((tk, hh * tq), F32),
                        pltpu.VMEM((rr + SUM_ROWS, hh * tq), F32),
                        pltpu.VMEM((rr, hh * tq), BF16),
                        pltpu.VMEM((tk, hh * tq), F32),
                        pltpu.VMEM((tk, hh * tq), BF16),
                        pltpu.VMEM((tk, tq), F32),
                        pltpu.VMEM((1, tq), I32)],
        compiler_params=_cparams(("arbitrary", "arbitrary")),
        name="dsa_attn",
    )(qt, qidxt, wsmt, kidx3, ckv3, ckvt4, wukt, wuvt)


def _mem_kv(mem, w_mem_kv):
    b, mlen, d = mem.shape
    (kv,) = _proj(mem.reshape(b * mlen, d), w_mem_kv.astype(BF16), (2 * MEM_W,), (BF16,))
    return kv.reshape(b, mlen, 2 * MEM_W)


def _gdn_layer(x3, mem, w_in, conv_w, a_log, dt_bias, onorm_g, w_mem_kv, w_o, g, b):
    bsz, t, d = x3.shape
    n_qkv = 3 * GDN_W
    n_gate = 4 * GDN_W
    n_small = n_gate + 2 * GDN_HEADS
    w = jnp.concatenate([w_in[:, :n_gate], _pad_cols(w_in[:, n_gate:n_small], LANES),
                         w_in[:, n_small:]], axis=1).astype(BF16)
    qkv, gate, small, q_mem = _proj(x3.reshape(bsz * t, d), w, (n_qkv, GDN_W, LANES, MEM_W),
                                    (F32, F32, F32, F32))
    o = _gdn_core(qkv.reshape(bsz, t, n_qkv), gate.reshape(bsz, t, GDN_W), small.reshape(bsz, t, LANES),
                  conv_w, a_log, dt_bias, onorm_g)
    return _mix_out(x3, o, q_mem.reshape(bsz, t, MEM_W), _mem_kv(mem, w_mem_kv), w_o, g, b,
                    o_transposed=False)


def _dsa_layer(x3, mem, w_in, kv_norm_g, w_uk, w_uv, kidx_g, kidx_b, w_mem_kv, w_o, g, b):
    bsz, t, d = x3.shape
    tk = min(DSA_TK, t)
    qt, ckv, ckvt, qidxt, kidx, wsmt, q_mem = _dsa_proj(x3.reshape(bsz * t, d), w_in, kv_norm_g,
                                                       kidx_g, kidx_b, tk)
    o_t = _dsa_attn(bsz, t, qt, qidxt, wsmt,
                    kidx.reshape(bsz, t, 2 * LANES), ckv.reshape(bsz, t, DSA_KV_RANK),
                    ckvt.reshape(bsz, t // tk, DSA_KV_RANK + SUM_ROWS, tk), w_uk, w_uv, tk)
    return _mix_out(x3, o_t, q_mem.reshape(bsz, t, MEM_W), _mem_kv(mem, w_mem_kv), w_o, g, b,
                    o_transposed=True)


def kernel(x, mem, l0_ffn1_w_in, l0_ffn1_w_out, l0_ln1_g, l0_ln1_b, l0_w_in, l0_conv_w, l0_a_log, l0_dt_bias, l0_onorm_g, l0_w_mem_kv, l0_w_o, l0_ln2_g, l0_ln2_b, l0_ffn2_w_in, l0_ffn2_w_out, l0_ln3_g, l0_ln3_b, l1_ffn1_w_in, l1_ffn1_w_out, l1_ln1_g, l1_ln1_b, l1_w_in, l1_kv_norm_g, l1_w_uk, l1_w_uv, l1_kidx_ln_g, l1_kidx_ln_b, l1_w_mem_kv, l1_w_o, l1_ln2_g, l1_ln2_b, l1_ffn2_w_in, l1_ffn2_w_out, l1_ln3_g, l1_ln3_b):
    bsz, t, d = x.shape

    def ffn(h3, w_in, w_out, g, b):
        return _ffn_ln(h3.reshape(bsz * t, d), w_in, w_out, g, b).reshape(bsz, t, d)

    h = ffn(x, l0_ffn1_w_in, l0_ffn1_w_out, l0_ln1_g, l0_ln1_b)
    h = _gdn_layer(h, mem, l0_w_in, l0_conv_w, l0_a_log, l0_dt_bias, l0_onorm_g, l0_w_mem_kv, l0_w_o,
                   l0_ln2_g, l0_ln2_b)
    h = ffn(h, l0_ffn2_w_in, l0_ffn2_w_out, l0_ln3_g, l0_ln3_b)
    h = ffn(h, l1_ffn1_w_in, l1_ffn1_w_out, l1_ln1_g, l1_ln1_b)
    h = _dsa_layer(h, mem, l1_w_in, l1_kv_norm_g, l1_w_uk, l1_w_uv, l1_kidx_ln_g, l1_kidx_ln_b,
                   l1_w_mem_kv, l1_w_o, l1_ln2_g, l1_ln2_b)
    h = ffn(h, l1_ffn2_w_in, l1_ffn2_w_out, l1_ln3_g, l1_ln3_b)
    return h
```

```python
import functools
import math

import jax
import jax.numpy as jnp
from jax import lax
from jax.experimental import pallas as pl
from jax.experimental.pallas import tpu as pltpu

F32 = jnp.float32
BF16 = jnp.bfloat16
I32 = jnp.int32

DEPTH = 2
LN_EPS = 1e-5
ALPHA = (2 * DEPTH) ** 0.25
GDN_HEADS = 6
GDN_DK = 128
GDN_W = GDN_HEADS * GDN_DK
GDN_CONV = 4
DSA_HEADS = 12
DSA_DH = 64
DSA_W = DSA_HEADS * DSA_DH
DSA_KV_RANK = 256
IDX_HEADS = 8
IDX_DIM = 64
TOPK_MAX = 256
MEM_HEADS = 4
MEM_DH = 64
MEM_W = MEM_HEADS * MEM_DH

LANES = 128
SUBLANES = 8
VMEM_LIMIT_BYTES = 56 * 1024 * 1024

ROW_TILE = 512
FFN_CHUNK = 256
GDN_CHUNK = 128
GDN_TILE = 256
INV_BLOCK = 16
DSA_TQ = 2 * LANES
DSA_TK = 512
NO_INDEX_BOUND = 2 ** 30
COUNT_ROWS = 4 * SUBLANES
SUM_ROWS = 2 * SUBLANES
ATT_SLAB = 4 * SUBLANES
RADIX_FIXED = 24
RADIX_CHECK = 4
LOG2E = math.log2(math.e)

def _cparams(sem):
    return pltpu.CompilerParams(dimension_semantics=sem, vmem_limit_bytes=VMEM_LIMIT_BYTES)


def _resident(shape):
    nd = len(shape)
    return pl.BlockSpec(shape, lambda *_: (0,) * nd, pipeline_mode=pl.Buffered(1))


def _mm(a, b):
    return jnp.dot(a.astype(BF16), b.astype(BF16), preferred_element_type=F32)


def _mm_nt(a, b):
    return lax.dot_general(a.astype(BF16), b.astype(BF16), (((1,), (1,)), ((), ())),
                           preferred_element_type=F32)


def _mm_tn(a, b):
    return lax.dot_general(a.astype(BF16), b.astype(BF16), (((0,), (0,)), ((), ())),
                           preferred_element_type=F32)


def _layer_norm(y, g, b):
    mu = jnp.mean(y, axis=-1, keepdims=True)
    d = y - mu
    var = jnp.mean(d * d, axis=-1, keepdims=True)
    return d * lax.rsqrt(var + LN_EPS) * g + b


def _silu(x):
    return x * jax.nn.sigmoid(x)


def _ffn_kernel(x_ref, wg_ref, wu_ref, wo_ref, g_ref, b_ref, o_ref, acc_ref, *, n_chunks, fc):
    x = x_ref[...]
    xb = x.astype(BF16)
    for c in range(n_chunks):
        cs = slice(c * fc, (c + 1) * fc)
        gate = jnp.dot(xb, wg_ref[:, cs], preferred_element_type=F32)
        up = jnp.dot(xb, wu_ref[:, cs], preferred_element_type=F32)
        part = jnp.dot((_silu(gate) * up).astype(BF16), wo_ref[cs, :], preferred_element_type=F32)
        if c == 0:
            acc_ref[...] = part
        else:
            acc_ref[...] += part
    o_ref[...] = _layer_norm(ALPHA * x + 0.5 * acc_ref[...], g_ref[...], b_ref[...])


def _prep_ffn(w_in, w_out):
    dff = w_out.shape[0]
    pad = -dff % FFN_CHUNK
    wg = jnp.pad(w_in[:, :dff].astype(BF16), ((0, 0), (0, pad)))
    wu = jnp.pad(w_in[:, dff:].astype(BF16), ((0, 0), (0, pad)))
    wo = jnp.pad(w_out.astype(BF16), ((0, pad), (0, 0)))
    return wg, wu, wo


def _ffn_ln(x2, w_in, w_out, g, b):
    m, d = x2.shape
    wg, wu, wo = _prep_ffn(w_in, w_out)
    bm = min(ROW_TILE, m)
    row = pl.BlockSpec((bm, d), lambda i: (i, 0))
    return pl.pallas_call(
        functools.partial(_ffn_kernel, n_chunks=wo.shape[0] // FFN_CHUNK, fc=FFN_CHUNK),
        grid=(m // bm,),
        in_specs=[row, _resident(wg.shape), _resident(wu.shape), _resident(wo.shape),
                  _resident((1, d)), _resident((1, d))],
        out_specs=row,
        out_shape=jax.ShapeDtypeStruct((m, d), F32),
        scratch_shapes=[pltpu.VMEM((bm, d), F32)],
        compiler_params=_cparams(("parallel",)),
        name="ffn_ln",
    )(x2, wg, wu, wo, g.reshape(1, d), b.reshape(1, d))


def _proj_kernel(x_ref, w_ref, *o_refs, widths):
    xb = x_ref[...].astype(BF16)
    off = 0
    for o_ref, n in zip(o_refs, widths):
        o_ref[...] = jnp.dot(xb, w_ref[:, off:off + n], preferred_element_type=F32).astype(o_ref.dtype)
        off += n


def _proj(x2, w, widths, dtypes):
    m, d = x2.shape
    bm = min(ROW_TILE, m)
    return pl.pallas_call(
        functools.partial(_proj_kernel, widths=widths),
        grid=(m // bm,),
        in_specs=[pl.BlockSpec((bm, d), lambda i: (i, 0)), _resident(w.shape)],
        out_specs=[pl.BlockSpec((bm, n), lambda i: (i, 0)) for n in widths],
        out_shape=[jax.ShapeDtypeStruct((m, n), dt) for n, dt in zip(widths, dtypes)],
        compiler_params=_cparams(("parallel",)),
        name="proj",
    )(x2, w)


def _gdn_proj_kernel(x_ref, w_ref, cw_ref, qkv_ref, gate_ref, small_ref, qm_ref, halo_ref, stage_ref,
                     *, tiles_per_batch):
    xb = x_ref[...].astype(BF16)
    bm = xb.shape[0]
    n_qkv = 3 * GDN_W

    def cols(lo, n):
        return jnp.dot(xb, w_ref[:, lo:lo + n], preferred_element_type=F32)

    @pl.when(pl.program_id(0) % tiles_per_batch == 0)
    def _():
        halo_ref[...] = jnp.zeros_like(halo_ref)

    gw = 2 * GDN_DK
    for grp in range(n_qkv // gw):
        cs = slice(grp * gw, (grp + 1) * gw)
        raw = cols(grp * gw, gw)
        stage_ref[0:SUBLANES, cs] = halo_ref[:, cs]
        stage_ref[SUBLANES:SUBLANES + bm, cs] = raw
        halo_ref[:, cs] = raw[bm - SUBLANES:bm, :]
        cw = cw_ref[:, cs]
        y = cw[GDN_CONV - 1:GDN_CONV, :] * raw
        for j in range(GDN_CONV - 1):
            lo = SUBLANES - (GDN_CONV - 1) + j
            y = y + cw[j:j + 1, :] * stage_ref[lo:lo + bm, cs]
        y = _silu(y)
        for half in range(2):
            h = 2 * grp + half
            yh = y[:, half * GDN_DK:(half + 1) * GDN_DK]
            if h < 2 * GDN_HEADS:
                yh = yh * lax.rsqrt(jnp.sum(yh * yh, axis=-1, keepdims=True) + 1e-6)
            if h < GDN_HEADS:
                yh = yh * (GDN_DK ** -0.5)
            qkv_ref[:, h * GDN_DK:(h + 1) * GDN_DK] = yh
    gate_ref[...] = cols(n_qkv, GDN_W)
    small_ref[...] = cols(n_qkv + GDN_W, LANES)
    qm_ref[...] = cols(n_qkv + GDN_W + LANES, MEM_W)


def _gdn_proj(x2, w, conv_w, tiles_per_batch_rows):
    m, d = x2.shape
    bm = min(ROW_TILE, tiles_per_batch_rows)
    widths = (3 * GDN_W, GDN_W, LANES, MEM_W)
    return pl.pallas_call(
        functools.partial(_gdn_proj_kernel, tiles_per_batch=tiles_per_batch_rows // bm),
        grid=(m // bm,),
        in_specs=[pl.BlockSpec((bm, d), lambda i: (i, 0)), _resident(w.shape), _resident(conv_w.shape)],
        out_specs=[pl.BlockSpec((bm, n), lambda i: (i, 0)) for n in widths],
        out_shape=[jax.ShapeDtypeStruct((m, n), F32) for n in widths],
        scratch_shapes=[pltpu.VMEM((SUBLANES, 3 * GDN_W), F32),
                        pltpu.VMEM((SUBLANES + bm, 3 * GDN_W), F32)],
        compiler_params=_cparams(("arbitrary",)),
        name="gdn_proj",
    )(x2, w, conv_w)


def _pad_cols(w, n):
    return jnp.pad(w, ((0, 0), (0, n - w.shape[1])))


def _unit_lower_inverse(mats, xor_idx, eye):
    c = mats[0].shape[0]
    shift = int(math.log2(INV_BLOCK))
    x = [jnp.where((xor_idx >> shift) == 0, a, 0.0) for a in mats]
    p = [eye - xi for xi in x]
    for _ in range(shift - 1):
        x = [_mm(xi, xi) for xi in x]
        p = [pi + _mm(pi, xi) for pi, xi in zip(p, x)]
    t = p
    while (1 << shift) < c:
        off = [jnp.where((xor_idx >> shift) == 1, a, 0.0) for a in mats]
        t_off = [_mm(ti, oi) for ti, oi in zip(t, off)]
        t = [ti - _mm(toi, ti) for ti, toi in zip(t, t_off)]
        shift += 1
    return t


def _gdn_kernel(alog_ref, dtb_ref, q_ref, k_ref, v_ref, gate_ref, small_ref, og_ref, o_ref,
                s_ref, smt_ref, *, chunk, n_chunks):
    tc = chunk * n_chunks

    @pl.when(pl.program_id(1) == 0)
    def _():
        s_ref[...] = jnp.zeros_like(s_ref)

    q_all = q_ref[0]
    k_all = k_ref[0]
    v_all = v_ref[0]

    smt_ref[...] = small_ref[0].T
    lane_in_chunk = lax.broadcasted_iota(I32, (SUBLANES, tc), 1) & (chunk - 1)

    def to_cols(row):
        return jnp.broadcast_to(row[0:1, :], (LANES, tc)).T

    ri = lax.broadcasted_iota(I32, (chunk, chunk), 0)
    ci = lax.broadcasted_iota(I32, (chunk, chunk), 1)
    incl = ri >= ci
    strict = ri > ci
    xor_idx = ri ^ ci
    eye = jnp.where(ri == ci, 1.0, 0.0).astype(F32)

    def each(f, *lists):
        return [f(*args) for args in zip(*lists)]

    heads = list(range(GDN_HEADS))
    col_of = [slice(h * GDN_DK, (h + 1) * GDN_DK) for h in heads]
    q_h = [q_all[:, cs] for cs in col_of]
    k_h = [k_all[:, cs] for cs in col_of]
    v_h = [v_all[:, cs] for cs in col_of]
    beta_row = [jax.nn.sigmoid(jnp.broadcast_to(smt_ref[h:h + 1, :], (SUBLANES, tc))) for h in heads]
    g_row = [-jnp.exp(jnp.full((SUBLANES, tc), alog_ref[h], F32))
             * jax.nn.softplus(jnp.broadcast_to(smt_ref[GDN_HEADS + h:GDN_HEADS + h + 1, :], (SUBLANES, tc))
                               + dtb_ref[h]) for h in heads]
    gc_row = g_row
    step = 1
    while step < chunk:
        gc_row = each(lambda g: g + jnp.where(lane_in_chunk >= step, pltpu.roll(g, step, axis=1), 0.0), gc_row)
        step *= 2
    beta_c = each(to_cols, beta_row)
    gc_c = each(to_cols, gc_row)

    probs = [(c, h) for c in range(n_chunks) for h in heads]
    rows_of = [slice(c * chunk, (c + 1) * chunk) for c, _ in probs]
    qc = [q_h[h][rows_of[i]] for i, (_, h) in enumerate(probs)]
    kc = [k_h[h][rows_of[i]] for i, (_, h) in enumerate(probs)]
    vc = [v_h[h][rows_of[i]] for i, (_, h) in enumerate(probs)]
    bc = [beta_c[h][rows_of[i]] for i, (_, h) in enumerate(probs)]
    gcc = [gc_c[h][rows_of[i]] for i, (_, h) in enumerate(probs)]
    gcr = [gc_row[h][0:1, rows_of[i]] for i, (_, h) in enumerate(probs)]
    decay = each(lambda a, b: jnp.where(incl, jnp.exp(jnp.where(incl, a - b, 0.0)), 0.0), gcc, gcr)
    kb = each(lambda a, b: a * b, kc, bc)
    a_mat = each(lambda a, b, d: jnp.where(strict, _mm_nt(a, b) * d, 0.0), kb, kc, decay)
    t_mat = _unit_lower_inverse(a_mat, xor_idx, eye)
    eg = each(jnp.exp, gcc)
    uw = each(lambda t, v, b, kbi, e: _mm(t, jnp.concatenate([v * b, kbi * e], axis=1)), t_mat, vc, bc, kb, eg)
    qk = each(lambda a, b, d: jnp.where(incl, _mm_nt(a, b) * d, 0.0), qc, kc, decay)
    g_last = each(lambda g: g[chunk - 1:chunk, :], gcc)
    k_dec = each(lambda k, gl, g: k * jnp.exp(gl - g), kc, g_last, gcc)
    lhs = each(lambda w, q, e: jnp.concatenate([w[:, GDN_DK:], q * e], axis=0), uw, qc, eg)

    state = [s_ref[h] for h in heads]
    for c in range(n_chunks):
        idx = [c * GDN_HEADS + h for h in heads]
        ws = [_mm(lhs[i], state[h]) for i, h in zip(idx, heads)]
        u = [uw[i][:, :GDN_DK] - w[:chunk] for i, w in zip(idx, ws)]
        o = [w[chunk:] + _mm(qk[i], ui) for i, w, ui in zip(idx, ws, u)]
        state = [state[h] * jnp.exp(g_last[i]) + _mm_tn(k_dec[i], ui) for i, h, ui in zip(idx, heads, u)]
        o = each(lambda x: x * lax.rsqrt(jnp.mean(x * x, axis=-1, keepdims=True) + LN_EPS) * og_ref[...], o)
        for h in heads:
            o_ref[0, rows_of[idx[h]], col_of[h]] = o[h] * _silu(gate_ref[0, rows_of[idx[h]], col_of[h]])
    for h in heads:
        s_ref[h] = state[h]


def _gdn_core(qkv, gate, small, a_log, dt_bias, onorm_g):
    b, t, _ = qkv.shape
    tc = min(GDN_TILE, t)
    n_chunks = tc // GDN_CHUNK
    hh = GDN_HEADS

    def tok_block(j):
        return pl.BlockSpec((1, tc, GDN_W), lambda bi, ti: (bi, ti, j))

    smem = pl.BlockSpec(memory_space=pltpu.SMEM)
    return pl.pallas_call(
        functools.partial(_gdn_kernel, chunk=GDN_CHUNK, n_chunks=n_chunks),
        grid=(b, t // tc),
        in_specs=[smem, smem,
                  tok_block(0), tok_block(1), tok_block(2),
                  tok_block(0),
                  pl.BlockSpec((1, tc, LANES), lambda bi, ti: (bi, ti, 0)),
                  pl.BlockSpec((1, GDN_DK), lambda bi, ti: (0, 0))],
        out_specs=tok_block(0),
        out_shape=jax.ShapeDtypeStruct((b, t, GDN_W), F32),
        scratch_shapes=[pltpu.VMEM((hh, GDN_DK, GDN_DK), F32),
                        pltpu.VMEM((LANES, tc), F32)],
        compiler_params=_cparams(("parallel", "arbitrary")),
        name="gdn_core",
    )(a_log, dt_bias, qkv, qkv, qkv, gate, small, onorm_g.reshape(1, GDN_DK))


def _mix_out_kernel(x_ref, o_ref, qm_ref, kv_ref, wo1_ref, wo2_ref, g_ref, b_ref, out_ref, *, o_transposed):
    qm = qm_ref[0]
    kv = kv_ref[0]
    k_mem, v_mem = kv[:, :MEM_W], kv[:, MEM_W:]
    head_of_lane = lax.broadcasted_iota(I32, (1, MEM_W), 1) >> (MEM_DH.bit_length() - 1)
    y_mix = _mm_tn(o_ref[...], wo1_ref[...]) if o_transposed else _mm(o_ref[0], wo1_ref[...])
    mine = [head_of_lane == hd for hd in range(MEM_HEADS)]
    s = [_mm_nt(jnp.where(mk, qm, 0.0), k_mem) * (MEM_DH ** -0.5) for mk in mine]
    s = [si - jnp.max(si, axis=-1, keepdims=True) for si in s]
    p = [jnp.exp(si) for si in s]
    p = [pi / jnp.sum(pi, axis=-1, keepdims=True) for pi in p]
    pv = [_mm(pi, v_mem) for pi in p]
    o_mem = jnp.zeros(qm.shape, F32)
    for mk, pvi in zip(mine, pv):
        o_mem = o_mem + jnp.where(mk, pvi, 0.0)
    y = y_mix + _mm(o_mem, wo2_ref[...])
    out_ref[0] = _layer_norm(ALPHA * x_ref[0] + y, g_ref[...], b_ref[...])


def _mix_out(x3, o, qm3, kv3, w_o, g, b, *, o_transposed):
    bsz, t, d = x3.shape
    w1 = o.shape[0] if o_transposed else o.shape[-1]
    bm = min(ROW_TILE, t)
    nt = t // bm
    wo1 = w_o[:w1].astype(BF16)
    wo2 = w_o[w1:].astype(BF16)

    def rows(n):
        return pl.BlockSpec((1, bm, n), lambda bi, ti: (bi, ti, 0))

    o_spec = pl.BlockSpec((w1, bm), lambda bi, ti: (0, bi * nt + ti)) if o_transposed else rows(w1)
    return pl.pallas_call(
        functools.partial(_mix_out_kernel, o_transposed=o_transposed),
        grid=(bsz, nt),
        in_specs=[rows(d), o_spec, rows(MEM_W),
                  pl.BlockSpec((1,) + kv3.shape[1:], lambda bi, ti: (bi, 0, 0)),
                  _resident(wo1.shape), _resident(wo2.shape), _resident((1, d)), _resident((1, d))],
        out_specs=rows(d),
        out_shape=jax.ShapeDtypeStruct((bsz, t, d), F32),
        compiler_params=_cparams(("parallel", "parallel")),
        name="mix_out",
    )(x3, o, qm3, kv3, wo1, wo2, g.reshape(1, d), b.reshape(1, d))


def _dsa_proj_kernel(x_ref, w_ref, kvg_ref, lng_ref, lnb_ref,
                     qt_ref, ckv_ref, ckvt_ref, qidxt_ref, kidx_ref, wsmt_ref, qm_ref, *, tk):
    xb = x_ref[...].astype(BF16)
    bm = xb.shape[0]

    def cols(lo, n):
        return jnp.dot(xb, w_ref[:, lo:lo + n], preferred_element_type=F32)

    o1 = DSA_W
    o2 = o1 + DSA_KV_RANK
    o3 = o2 + IDX_HEADS * IDX_DIM
    qt_ref[...] = cols(0, o1).T.astype(qt_ref.dtype)
    c = cols(o1, DSA_KV_RANK)
    c = c * lax.rsqrt(jnp.mean(c * c, axis=-1, keepdims=True) + LN_EPS) * kvg_ref[...]
    ckv_ref[...] = c.astype(ckv_ref.dtype)
    c_t = c.T.astype(ckvt_ref.dtype)
    for j in range(bm // tk):
        ckvt_ref[j, 0:DSA_KV_RANK, :] = c_t[:, j * tk:(j + 1) * tk]
        ckvt_ref[j, DSA_KV_RANK:, :] = jnp.ones((SUM_ROWS, tk), ckvt_ref.dtype)
    qidxt_ref[...] = cols(o2, IDX_HEADS * IDX_DIM).T.astype(qidxt_ref.dtype)
    lane = lax.broadcasted_iota(I32, (1, LANES), 1)
    for half in range(2):
        kx = cols(o3 + half * LANES, LANES)
        live = (lane >= half * IDX_DIM) & (lane < (half + 1) * IDX_DIM)
        mu = jnp.sum(kx, axis=-1, keepdims=True) * (1.0 / IDX_DIM)
        d = jnp.where(live, kx - mu, 0.0)
        var = jnp.sum(d * d, axis=-1, keepdims=True) * (1.0 / IDX_DIM)
        kn = d * lax.rsqrt(var + LN_EPS) * lng_ref[:, half * LANES:(half + 1) * LANES] \
            + lnb_ref[:, half * LANES:(half + 1) * LANES]
        kidx_ref[:, half * LANES:(half + 1) * LANES] = jnp.where(live, kn, 0.0).astype(kidx_ref.dtype)
    w_t = (cols(o3 + 2 * LANES, LANES) * (IDX_HEADS ** -0.5 * IDX_DIM ** -0.5)).T
    wsmt_ref[...] = w_t[:IDX_HEADS, :]
    qm_ref[...] = cols(o3 + 3 * LANES, MEM_W)


def _dsa_proj(x2, w_in, kv_norm_g, kidx_g, kidx_b, tk):
    m, d = x2.shape
    o1 = DSA_W
    o2 = o1 + DSA_KV_RANK
    o3 = o2 + IDX_HEADS * IDX_DIM
    o4 = o3 + IDX_DIM
    o5 = o4 + IDX_HEADS
    zeros64 = jnp.zeros((d, IDX_DIM), w_in.dtype)
    w = jnp.concatenate([
        w_in[:, :o3],
        w_in[:, o3:o4], zeros64, zeros64, w_in[:, o3:o4],
        _pad_cols(w_in[:, o4:o5], LANES),
        w_in[:, o5:],
    ], axis=1).astype(BF16)
    z64 = jnp.zeros((IDX_DIM,), F32)
    lng = jnp.concatenate([kidx_g, z64, z64, kidx_g]).reshape(1, 2 * LANES)
    lnb = jnp.concatenate([kidx_b, z64, z64, kidx_b]).reshape(1, 2 * LANES)
    bm = min(ROW_TILE, m)
    nqi = IDX_HEADS * IDX_DIM
    rr = DSA_KV_RANK

    def tok_major(n):
        return pl.BlockSpec((bm, n), lambda i: (i, 0))

    def feat_major(n):
        return pl.BlockSpec((n, bm), lambda i: (0, i))

    return pl.pallas_call(
        functools.partial(_dsa_proj_kernel, tk=tk),
        grid=(m // bm,),
        in_specs=[tok_major(d), _resident(w.shape),
                  _resident((1, rr)), _resident((1, 2 * LANES)), _resident((1, 2 * LANES))],
        out_specs=[feat_major(DSA_W), tok_major(rr),
                   pl.BlockSpec((bm // tk, rr + SUM_ROWS, tk), lambda i: (i, 0, 0)),
                   feat_major(nqi), tok_major(2 * LANES), feat_major(IDX_HEADS), tok_major(MEM_W)],
        out_shape=[jax.ShapeDtypeStruct((DSA_W, m), BF16),
                   jax.ShapeDtypeStruct((m, rr), BF16),
                   jax.ShapeDtypeStruct((m // tk, rr + SUM_ROWS, tk), BF16),
                   jax.ShapeDtypeStruct((nqi, m), BF16),
                   jax.ShapeDtypeStruct((m, 2 * LANES), BF16),
                   jax.ShapeDtypeStruct((IDX_HEADS, m), F32),
                   jax.ShapeDtypeStruct((m, MEM_W), F32)],
        compiler_params=_cparams(("parallel",)),
        name="dsa_proj",
    )(x2, w, kv_norm_g.reshape(1, rr), lng, lnb)


def _ordered_key_to_f32(u):
    key = u ^ jnp.int32(-2 ** 31)
    bits = key ^ ((key >> 31) & jnp.int32(0x7FFFFFFF))
    return lax.bitcast_convert_type(bits, F32)


def _dsa_attn_kernel(qt_ref, qidxt_ref, wsmt_ref, kidx_ref, ckv_ref, ckvt_ref, wukt_ref, wuvt_ref, o_ref,
                     sc_ref, b0_ref, acc_ref, qlat_ref, x_ref, p_ref, bias_ref, pos_ref,
                     *, tq, tk, topk):
    hh, rr, dh = DSA_HEADS, DSA_KV_RANK, DSA_DH
    t0 = pl.program_id(1) * tq
    nk = (t0 + tq + tk - 1) // tk
    key_i = lax.broadcasted_iota(I32, (tk, tq), 0)
    t_pos = t0 + lax.broadcasted_iota(I32, (tk, tq), 1)
    neg_inf = jnp.float32(-jnp.inf)

    pairs = IDX_HEADS // 2
    lhs_t = jnp.concatenate([qidxt_ref[p * LANES:(p + 1) * LANES, :] for p in range(pairs)], axis=1)
    w_t = wsmt_ref[...]
    w_rows = [jnp.concatenate([w_t[2 * p + half:2 * p + half + 1, :] for p in range(pairs)], axis=1)
              for half in range(2)]

    def idx_body(kb, carry):
        k0 = pl.multiple_of(kb * tk, tk)
        kblk = kidx_ref[0, pl.ds(k0, tk), :]
        acc = jnp.zeros((tk, tq), F32)
        for half in range(2):
            z = jnp.dot(kblk[:, half * LANES:(half + 1) * LANES], lhs_t, preferred_element_type=F32)
            z = jnp.maximum(z, 0.0) * w_rows[half]
            for p in range(pairs):
                acc = acc + z[:, p * tq:(p + 1) * tq]
        sc_ref[kb] = jnp.where(k0 + key_i <= t_pos, acc, neg_inf)
        return carry

    lax.fori_loop(0, nk, idx_body, 0)

    n_valid = (t0 + lax.broadcasted_iota(I32, (1, tq), 1) + 1).astype(F32)
    k_row = jnp.minimum(jnp.float32(topk), n_valid)

    slab_i = lax.broadcasted_iota(I32, (COUNT_ROWS, tq), 0)

    def count(pred):
        def body(kb, acc):
            for j in range(tk // COUNT_ROWS):
                row0 = j * COUNT_ROWS
                slab = sc_ref[kb, row0:row0 + COUNT_ROWS, :]
                acc = acc + jnp.where(pred(slab, kb * tk + row0), jnp.float32(1.0), jnp.float32(0.0))
            return acc
        acc = lax.fori_loop(0, nk, body, jnp.zeros((COUNT_ROWS, tq), F32))
        return jnp.sum(acc, axis=0, keepdims=True)

    def radix_step(it, c):
        tau_u, done = c
        cand_u = tau_u | jnp.left_shift(jnp.int32(1), 31 - it)
        cand = _ordered_key_to_f32(cand_u)
        cnt = count(lambda slab, key0: slab >= cand)
        tau_u = jnp.where((done == 0.0) & (cnt >= k_row), cand_u, tau_u)
        return tau_u, jnp.where(cnt == k_row, 1.0, done)

    def radix_chunk(c):
        it, tau_u, done, _ = c
        tau_u, done = lax.fori_loop(0, RADIX_CHECK, lambda j, s: radix_step(it + j, s), (tau_u, done))
        return it + RADIX_CHECK, tau_u, done, jnp.sum(1.0 - done)

    cnt_pos = count(lambda slab, key0: slab > 0.0)
    cnt_nonneg = count(lambda slab, key0: slab >= 0.0)
    zero_tie = (cnt_pos < k_row) & (cnt_nonneg > k_row)
    tau_u = jnp.where(cnt_nonneg >= k_row, jnp.int32(-2 ** 31), jnp.int32(0))
    done = jnp.where((cnt_nonneg == k_row) | zero_tie, 1.0, 0.0).astype(F32)
    tau_u, done = lax.fori_loop(1, RADIX_FIXED, radix_step, (tau_u, done))
    _, tau_u, _, n_open = lax.while_loop(
        lambda c: (c[0] < 32) & (c[3] > 0.0), radix_chunk,
        (jnp.int32(RADIX_FIXED), tau_u, done, jnp.sum(1.0 - done)))
    tau = _ordered_key_to_f32(tau_u)

    pos_ref[...] = jnp.full((1, tq), NO_INDEX_BOUND, I32)

    @pl.when(n_open + jnp.sum(jnp.where(zero_tie, jnp.float32(1.0), jnp.float32(0.0))) > 0.0)
    def _():
        need = k_row - count(lambda slab, key0: slab > tau)
        ri = lax.broadcasted_iota(I32, (tk, tk), 0)
        ci = lax.broadcasted_iota(I32, (tk, tk), 1)
        lower = jnp.where(ri >= ci, jnp.float32(1.0), jnp.float32(0.0)).astype(BF16)

        def tie_body(kb, c):
            seen, below = c
            ties = jnp.where(sc_ref[kb] == tau, jnp.float32(1.0), jnp.float32(0.0)).astype(BF16)
            running = jnp.dot(lower, ties, preferred_element_type=F32) + seen
            below = below + jnp.sum(jnp.where(running < need, jnp.float32(1.0), jnp.float32(0.0)),
                                    axis=0, keepdims=True)
            return running[tk - 1:tk, :], below

        _, below = lax.fori_loop(0, nk, tie_body, (jnp.zeros((1, tq), F32), jnp.zeros((1, tq), F32)))
        pos_ref[...] = below.astype(I32)

    pos = pos_ref[...]

    qt = qt_ref[...]
    q_lat_t = jnp.concatenate(
        [jnp.dot(wukt_ref[hd], qt[hd * dh:(hd + 1) * dh, :], preferred_element_type=F32)
         for hd in range(hh)], axis=1)
    qlat_ref[...] = (q_lat_t * (dh ** -0.5 * LOG2E)).astype(BF16)
    slopes = [2.0 ** (-8.0 * (hd + 1) / hh) * LOG2E for hd in range(hh)]
    slope_row = jnp.concatenate([jnp.full((1, tq), sl, F32) for sl in slopes], axis=1)
    @pl.when((pl.program_id(0) == 0) & (pl.program_id(1) == 0))
    def _():
        key_f = key_i.astype(F32)
        for hd in range(hh):
            b0_ref[:, hd * tq:(hd + 1) * tq] = key_f * slopes[hd]
    n_slabs = tk // ATT_SLAB

    def att_body(kb, m_old):
        k0 = pl.multiple_of(kb * tk, tk)
        c_blk = ckv_ref[0, pl.ds(k0, tk), :]
        blk = sc_ref[kb]
        sel = (blk > tau) | ((blk == tau) & (k0 + key_i <= pos))
        bias_ref[...] = jnp.where(sel, 0.0, neg_inf)
        r_all = slope_row * (k0 - t0).astype(F32)
        m_parts, a_parts = [], []
        for hd in range(hh):
            cs = slice(hd * tq, (hd + 1) * tq)
            x_ref[:, cs] = jnp.dot(c_blk, qlat_ref[:, cs], preferred_element_type=F32)
            mx = jnp.full((ATT_SLAB, tq), neg_inf, F32)
            for j in range(n_slabs):
                rows = slice(j * ATT_SLAB, (j + 1) * ATT_SLAB)
                xj = x_ref[rows, cs] + b0_ref[rows, cs] + bias_ref[rows, :]
                x_ref[rows, cs] = xj
                mx = jnp.maximum(mx, xj)
            r = r_all[:, cs]
            m_o = m_old[:, cs]
            m_n = jnp.maximum(m_o, jnp.max(mx, axis=0, keepdims=True) + r)
            m_s = jnp.where(m_n == neg_inf, 0.0, m_n)
            shift = m_s - r
            for j in range(n_slabs):
                rows = slice(j * ATT_SLAB, (j + 1) * ATT_SLAB)
                p_ref[rows, cs] = jnp.exp2(x_ref[rows, cs] - shift).astype(BF16)
            m_parts.append(m_n)
            a_parts.append(jnp.exp2(m_o - m_s))
        acc_ref[...] = jnp.concatenate(a_parts, axis=1) * acc_ref[...] + jnp.dot(
            ckvt_ref[0, kb], p_ref[...], preferred_element_type=F32)
        return jnp.concatenate(m_parts, axis=1)

    acc_ref[...] = jnp.zeros(acc_ref.shape, F32)
    lax.fori_loop(0, nk, att_body, jnp.full((1, hh * tq), neg_inf, F32))
    o_lat_t = (acc_ref[0:rr, :] / acc_ref[rr:rr + 1, :]).astype(BF16)
    for hd in range(hh):
        o_ref[hd * dh:(hd + 1) * dh, :] = jnp.dot(
            wuvt_ref[hd], o_lat_t[:, hd * tq:(hd + 1) * tq], preferred_element_type=F32).astype(o_ref.dtype)


def _dsa_attn(b, t, qt, qidxt, wsmt, kidx3, ckv3, ckvt4, w_uk, w_uv, tk):
    tq = min(DSA_TQ, t)
    nq = t // tq
    topk = min(TOPK_MAX, t // 4)
    hh, rr = DSA_HEADS, DSA_KV_RANK
    wukt = w_uk.transpose(0, 2, 1).astype(BF16)
    wuvt = w_uv.transpose(0, 2, 1).astype(BF16)

    def qcols(n):
        return pl.BlockSpec((n, tq), lambda bi, ti: (0, bi * nq + ti))

    return pl.pallas_call(
        functools.partial(_dsa_attn_kernel, tq=tq, tk=tk, topk=topk),
        grid=(b, nq),
        in_specs=[qcols(DSA_W), qcols(IDX_HEADS * IDX_DIM), qcols(IDX_HEADS),
                  pl.BlockSpec((1, t, 2 * LANES), lambda bi, ti: (bi, 0, 0)),
                  pl.BlockSpec((1, t, rr), lambda bi, ti: (bi, 0, 0)),
                  pl.BlockSpec((1, t // tk, rr + SUM_ROWS, tk), lambda bi, ti: (bi, 0, 0, 0)),
                  _resident(wukt.shape), _resident(wuvt.shape)],
        out_specs=qcols(DSA_W),
        out_shape=jax.ShapeDtypeStruct((DSA_W, b * t), BF16),
        scratch_shapes=[pltpu.VMEM((t // tk, tk, tq), F32),
                        pltpu.VMEM((tk, hh * tq), F32),
                        pltpu.VMEM((rr + SUM_ROWS, hh * tq), F32),
                        pltpu.VMEM((rr, hh * tq), BF16),
                        pltpu.VMEM((tk, hh * tq), F32),
                        pltpu.VMEM((tk, hh * tq), BF16),
                        pltpu.VMEM((tk, tq), F32),
                        pltpu.VMEM((1, tq), I32)],
        compiler_params=_cparams(("arbitrary", "arbitrary")),
        name="dsa_attn",
    )(qt, qidxt, wsmt, kidx3, ckv3, ckvt4, wukt, wuvt)


def _mem_kv(mem, w_mem_kv):
    b, mlen, d = mem.shape
    (kv,) = _proj(mem.reshape(b * mlen, d), w_mem_kv.astype(BF16), (2 * MEM_W,), (BF16,))
    return kv.reshape(b, mlen, 2 * MEM_W)


def _gdn_layer(x3, mem, w_in, conv_w, a_log, dt_bias, onorm_g, w_mem_kv, w_o, g, b):
    bsz, t, d = x3.shape
    n_qkv = 3 * GDN_W
    n_gate = 4 * GDN_W
    n_small = n_gate + 2 * GDN_HEADS
    w = jnp.concatenate([w_in[:, :n_gate], _pad_cols(w_in[:, n_gate:n_small], LANES),
                         w_in[:, n_small:]], axis=1).astype(BF16)
    qkv, gate, small, q_mem = _gdn_proj(x3.reshape(bsz * t, d), w, conv_w, t)
    o = _gdn_core(qkv.reshape(bsz, t, n_qkv), gate.reshape(bsz, t, GDN_W), small.reshape(bsz, t, LANES),
                  a_log, dt_bias, onorm_g)
    return _mix_out(x3, o, q_mem.reshape(bsz, t, MEM_W), _mem_kv(mem, w_mem_kv), w_o, g, b,
                    o_transposed=False)


def _dsa_layer(x3, mem, w_in, kv_norm_g, w_uk, w_uv, kidx_g, kidx_b, w_mem_kv, w_o, g, b):
    bsz, t, d = x3.shape
    tk = min(DSA_TK, t)
    qt, ckv, ckvt, qidxt, kidx, wsmt, q_mem = _dsa_proj(x3.reshape(bsz * t, d), w_in, kv_norm_g,
                                                       kidx_g, kidx_b, tk)
    o_t = _dsa_attn(bsz, t, qt, qidxt, wsmt,
                    kidx.reshape(bsz, t, 2 * LANES), ckv.reshape(bsz, t, DSA_KV_RANK),
                    ckvt.reshape(bsz, t // tk, DSA_KV_RANK + SUM_ROWS, tk), w_uk, w_uv, tk)
    return _mix_out(x3, o_t, q_mem.reshape(bsz, t, MEM_W), _mem_kv(mem, w_mem_kv), w_o, g, b,
                    o_transposed=True)


def kernel(x, mem, l0_ffn1_w_in, l0_ffn1_w_out, l0_ln1_g, l0_ln1_b, l0_w_in, l0_conv_w, l0_a_log, l0_dt_bias, l0_onorm_g, l0_w_mem_kv, l0_w_o, l0_ln2_g, l0_ln2_b, l0_ffn2_w_in, l0_ffn2_w_out, l0_ln3_g, l0_ln3_b, l1_ffn1_w_in, l1_ffn1_w_out, l1_ln1_g, l1_ln1_b, l1_w_in, l1_kv_norm_g, l1_w_uk, l1_w_uv, l1_kidx_ln_g, l1_kidx_ln_b, l1_w_mem_kv, l1_w_o, l1_ln2_g, l1_ln2_b, l1_ffn2_w_in, l1_ffn2_w_out, l1_ln3_g, l1_ln3_b):
    bsz, t, d = x.shape

    def ffn(h3, w_in, w_out, g, b):
        return _ffn_ln(h3.reshape(bsz * t, d), w_in, w_out, g, b).reshape(bsz, t, d)

    h = ffn(x, l0_ffn1_w_in, l0_ffn1_w_out, l0_ln1_g, l0_ln1_b)
    h = _gdn_layer(h, mem, l0_w_in, l0_conv_w, l0_a_log, l0_dt_bias, l0_onorm_g, l0_w_mem_kv, l0_w_o,
                   l0_ln2_g, l0_ln2_b)
    h = ffn(h, l0_ffn2_w_in, l0_ffn2_w_out, l0_ln3_g, l0_ln3_b)
    h = ffn(h, l1_ffn1_w_in, l1_ffn1_w_out, l1_ln1_g, l1_ln1_b)
    h = _dsa_layer(h, mem, l1_w_in, l1_kv_norm_g, l1_w_uk, l1_w_uv, l1_kidx_ln_g, l1_kidx_ln_b,
                   l1_w_mem_kv, l1_w_o, l1_ln2_g, l1_ln2_b)
    h = ffn(h, l1_ffn2_w_in, l1_ffn2_w_out, l1_ln3_g, l1_ln3_b)
    return h
```
